```python
import jax, jax.numpy as jnp
from jax import lax
import numpy as np

D_MODEL = 4096
BATCH = 4
SEQ = 2048
DEPTH = 1
DEC_BATCH = 128
DEC_SEQ = 1
PAST_LEN = 16384
PAGE_SIZE = 128

N_META = 16
GLA_HEADS = 8
GLA_DK = 256
GLA_DV = 512
GLA_QK_WIDTH = GLA_HEADS * GLA_DK
GLA_V_WIDTH = GLA_HEADS * GLA_DV
GATE_RANK = 16
GATE_TAU = 16.0
GLA_CHUNK = 64
CONV_WIDTH = D_MODEL
CONV_K = 3
N_EXPERTS = 256
TOP_K = 8
N_GROUPS = 8
TOPK_GROUPS = 4
EXPERT_DIM = 1024
SHARED_DIM = 1024
ROUTED_SCALE = 2.5
MOE_BLOCK = 128
MOE_BLOCK_SMALL = 16
ALPHA = (2.0 * DEPTH) ** 0.25
BETA = (8.0 * DEPTH) ** -0.25
EPS = 1e-5
IN_WIDTHS = (GLA_QK_WIDTH, GLA_QK_WIDTH, GLA_V_WIDTH, GATE_RANK, GLA_V_WIDTH,
             CONV_WIDTH, CONV_WIDTH, CONV_WIDTH, D_MODEL, D_MODEL)
N_IN = int(sum(IN_WIDTHS))
IN_SPLITS = [int(s) for s in np.cumsum(IN_WIDTHS)[:-1]]

kernel_name = "gla_shortconv_gated_moe_deepnorm_step"


def layer_norm(x, g, b):
    xf = x.astype(jnp.float32)
    mu = jnp.mean(xf, axis=-1, keepdims=True)
    var = jnp.mean(jnp.square(xf - mu), axis=-1, keepdims=True)
    return ((xf - mu) * lax.rsqrt(var + EPS) * g + b).astype(x.dtype)


def swiglu(x, wg, wu, wd):
    return (jax.nn.silu(x @ wg) * (x @ wu)) @ wd


def gla_scan(q, k, v, la, s0, chunk):
    bsz, length, h, dk = q.shape
    dv = v.shape[-1]
    n = length // chunk
    f32 = jnp.float32

    def to_chunks(a):
        return a.astype(f32).reshape(bsz, n, chunk, h, a.shape[-1]).transpose(1, 0, 3, 2, 4)

    causal = jnp.tril(jnp.ones((chunk, chunk), dtype=bool))

    def step(S, inp):
        qc, kc, vc, lc = inp
        b = jnp.cumsum(lc, axis=2)
        o_inter = jnp.einsum('bhik,bhkv->bhiv', qc * jnp.exp(b), S)
        rel = jnp.where(causal[None, None, :, :, None],
                        b[:, :, :, None, :] - b[:, :, None, :, :], -jnp.inf)
        scores = jnp.einsum('bhik,bhjk,bhijk->bhij', qc, kc, jnp.exp(rel))
        o = o_inter + jnp.einsum('bhij,bhjv->bhiv', scores, vc)
        b_last = b[:, :, -1:, :]
        S_new = jnp.exp(b_last[:, :, 0, :])[..., None] * S + \
            jnp.einsum('bhjk,bhjv->bhkv', kc * jnp.exp(b_last - b), vc)
        return S_new, o

    S, o = lax.scan(step, s0.astype(f32), (to_chunks(q), to_chunks(k), to_chunks(v), to_chunks(la)))
    o = o.transpose(1, 0, 3, 2, 4).reshape(bsz, length, h, dv)
    return o.astype(v.dtype), S.astype(s0.dtype)


def mixer(xn, gla_s0, conv_prefix, splits, w_in, w_alpha_up, b_alpha, conv_w, gla_norm_g, w_o):
    bsz, length, _ = xn.shape
    proj = xn @ w_in
    q, k, v, lr, g, cb, cc, ch, ga, gc = jnp.split(proj, IN_SPLITS, axis=-1)
    q = q.reshape(bsz, length, GLA_HEADS, GLA_DK) * (GLA_DK ** -0.5)
    k = k.reshape(bsz, length, GLA_HEADS, GLA_DK)
    v = v.reshape(bsz, length, GLA_HEADS, GLA_DV)
    la = jax.nn.log_sigmoid((lr @ w_alpha_up + b_alpha).astype(jnp.float32)) / GATE_TAU
    la = la.reshape(bsz, length, GLA_HEADS, GLA_DK)
    S = gla_s0
    outs = []
    start = 0
    for seg_len, chunk in splits:
        sl = slice(start, start + seg_len)
        o_seg, S = gla_scan(q[:, sl], k[:, sl], v[:, sl], la[:, sl], S, chunk)
        outs.append(o_seg)
        start += seg_len
    o = jnp.concatenate(outs, axis=1) if len(outs) > 1 else outs[0]
    of = o.astype(jnp.float32)
    of = of * lax.rsqrt(jnp.mean(jnp.square(of), axis=-1, keepdims=True) + EPS) * \
        gla_norm_g.reshape(GLA_HEADS, GLA_DV)
    o_gla = of.reshape(bsz, length, GLA_V_WIDTH).astype(xn.dtype) * jax.nn.silu(g)
    u = cc * ch
    u_ext = jnp.concatenate([conv_prefix.astype(u.dtype), u], axis=1)
    yconv = sum(conv_w[j] * u_ext[:, j:j + length] for j in range(CONV_K))
    o_conv = cb * yconv
    merged = jax.nn.sigmoid(ga) * o_gla + jax.nn.sigmoid(gc) * o_conv
    return merged @ w_o, S, u_ext[:, -(CONV_K - 1):]


def routed_experts(xt, eidx, wts, w_gate, w_up, w_down):
    n_tok = xt.shape[0]
    n_assign = n_tok * TOP_K
    blk = MOE_BLOCK if n_assign >= N_EXPERTS * MOE_BLOCK else MOE_BLOCK_SMALL
    e_flat = eidx.reshape(-1)
    tok_flat = jnp.arange(n_assign, dtype=jnp.int32) // TOP_K
    w_flat = wts.reshape(-1)
    counts = jnp.zeros((N_EXPERTS,), jnp.int32).at[e_flat].add(1)
    padded = (counts + blk - 1) // blk * blk
    start_sorted = jnp.cumsum(counts) - counts
    pad_end = jnp.cumsum(padded)
    pad_start = pad_end - padded
    order = jnp.argsort(e_flat, stable=True)
    e_s = e_flat[order]
    dest = pad_start[e_s] + jnp.arange(n_assign, dtype=jnp.int32) - start_sorted[e_s]
    n_blocks = -(-(n_assign + N_EXPERTS * (blk - 1)) // blk)
    n_slots = n_blocks * blk
    slot_tok = jnp.full((n_slots,), n_tok, jnp.int32).at[dest].set(tok_flat[order])
    slot_w = jnp.zeros((n_slots,), xt.dtype).at[dest].set(w_flat[order])
    blk_expert = jnp.minimum(
        jnp.searchsorted(pad_end, jnp.arange(n_blocks, dtype=jnp.int32) * blk, side='right'),
        N_EXPERTS - 1)
    x_pad = jnp.concatenate([xt, jnp.zeros((1, xt.shape[1]), xt.dtype)], axis=0)
    xb = x_pad[slot_tok].reshape(n_blocks, blk, xt.shape[1])

    def one_block(args):
        xblk, e = args
        return swiglu(xblk, w_gate[e], w_up[e], w_down[e])

    yb = lax.map(one_block, (xb, blk_expert))
    y = jax.ops.segment_sum(yb.reshape(n_slots, -1) * slot_w[:, None], slot_tok,
                            num_segments=n_tok + 1)
    return y[:n_tok]


def moe(x, w_router, b_router, w_exp_gate, w_exp_up, w_exp_down, w_sh_gate, w_sh_up, w_sh_down):
    shp = x.shape
    xt = x.reshape(-1, shp[-1])
    n_tok = xt.shape[0]
    scores = jax.nn.sigmoid((xt @ w_router).astype(jnp.float32))
    choice = scores + b_router.astype(jnp.float32)
    grp = choice.reshape(n_tok, N_GROUPS, N_EXPERTS // N_GROUPS)
    grp_score = lax.top_k(grp, 2)[0].sum(-1)
    _, gidx = lax.top_k(grp_score, TOPK_GROUPS)
    gmask = (gidx[..., None] == jnp.arange(N_GROUPS)).any(axis=1)
    masked = jnp.where(jnp.repeat(gmask, N_EXPERTS // N_GROUPS, axis=1), choice, -jnp.inf)
    _, eidx = lax.top_k(masked, TOP_K)
    w = jnp.take_along_axis(scores, eidx, axis=1)
    w = w / jnp.sum(w, axis=-1, keepdims=True) * ROUTED_SCALE
    routed = routed_experts(xt, eidx, w.astype(xt.dtype), w_exp_gate, w_exp_up, w_exp_down)
    shared = swiglu(xt, w_sh_gate, w_sh_up, w_sh_down)
    return (routed + shared).reshape(shp)


def block(x, gla_s0, conv_prefix, splits, w_in, w_alpha_up, b_alpha, conv_w, gla_norm_g, w_o,
          ln1_g, ln1_b, w_router, b_router, w_exp_gate, w_exp_up, w_exp_down,
          w_sh_gate, w_sh_up, w_sh_down, ln2_g, ln2_b):
    h, S, buf = mixer(x, gla_s0, conv_prefix, splits, w_in, w_alpha_up, b_alpha, conv_w, gla_norm_g, w_o)
    x = layer_norm(ALPHA * x + h, ln1_g, ln1_b)
    f = moe(x, w_router, b_router, w_exp_gate, w_exp_up, w_exp_down, w_sh_gate, w_sh_up, w_sh_down)
    x = layer_norm(ALPHA * x + f, ln2_g, ln2_b)
    return x, S, buf


def setup_inputs(seed: int = 0) -> dict:
    key = jax.random.key(seed)
    ks = jax.random.split(key, 28)
    f32 = jnp.float32
    D, E, F, FS = D_MODEL, N_EXPERTS, EXPERT_DIM, SHARED_DIM

    def nrm(k, shape, scale=1.0):
        return jax.random.normal(k, shape, f32) * scale

    def gain(k, shape):
        return 1.0 + 0.01 * jax.random.normal(k, shape, f32)

    return {
        "x_prompt": nrm(ks[0], (BATCH, SEQ, D)),
        "x_sample": nrm(ks[1], (DEC_BATCH, DEC_SEQ, D)),
        "state_gla": nrm(ks[2], (DEPTH, DEC_BATCH, GLA_HEADS, GLA_DK, GLA_DV), 0.5),
        "state_conv": nrm(ks[3], (DEPTH, DEC_BATCH, CONV_K - 1, CONV_WIDTH)),
        "meta_tokens": nrm(ks[4], (N_META, D)),
        "ln_emb_g": gain(ks[5], (D,)),
        "ln_emb_b": nrm(ks[6], (D,), 0.01),
        "w_in": nrm(ks[7], (DEPTH, D, N_IN), D ** -0.5),
        "w_alpha_up": nrm(ks[8], (DEPTH, GATE_RANK, GLA_QK_WIDTH), GATE_RANK ** -0.5),
        "b_alpha": nrm(ks[9], (DEPTH, GLA_QK_WIDTH), 0.1),
        "conv_w": nrm(ks[10], (DEPTH, CONV_K, CONV_WIDTH), CONV_K ** -0.5),
        "gla_norm_g": gain(ks[11], (DEPTH, GLA_V_WIDTH)),
        "w_o": nrm(ks[12], (DEPTH, D, D), BETA * D ** -0.5),
        "ln1_g": gain(ks[13], (DEPTH, D)),
        "ln1_b": nrm(ks[14], (DEPTH, D), 0.01),
        "w_router": nrm(ks[15], (DEPTH, D, E), D ** -0.5),
        "b_router": nrm(ks[16], (DEPTH, E), 0.01),
        "w_exp_gate": nrm(ks[17], (DEPTH, E, D, F), D ** -0.5),
        "w_exp_up": nrm(ks[18], (DEPTH, E, D, F), D ** -0.5),
        "w_exp_down": nrm(ks[19], (DEPTH, E, F, D), BETA * F ** -0.5),
        "w_sh_gate": nrm(ks[20], (DEPTH, D, FS), D ** -0.5),
        "w_sh_up": nrm(ks[21], (DEPTH, D, FS), D ** -0.5),
        "w_sh_down": nrm(ks[22], (DEPTH, FS, D), BETA * FS ** -0.5),
        "ln2_g": gain(ks[23], (DEPTH, D)),
        "ln2_b": nrm(ks[24], (DEPTH, D), 0.01),
    }


def reference(x_prompt, x_sample, state_gla, state_conv, meta_tokens, ln_emb_g, ln_emb_b,
              w_in, w_alpha_up, b_alpha, conv_w, gla_norm_g, w_o, ln1_g, ln1_b,
              w_router, b_router, w_exp_gate, w_exp_up, w_exp_down,
              w_sh_gate, w_sh_up, w_sh_down, ln2_g, ln2_b):
    bp = x_prompt.shape[0]
    seq = x_prompt.shape[1]
    dec_seq = x_sample.shape[1]
    meta = jnp.broadcast_to(meta_tokens[None].astype(x_prompt.dtype), (bp, N_META, D_MODEL))
    xp = layer_norm(jnp.concatenate([meta, x_prompt], axis=1), ln_emb_g, ln_emb_b)
    xs = layer_norm(x_sample, ln_emb_g, ln_emb_b)
    prompt_splits = ((N_META, N_META), (seq, GLA_CHUNK))
    sample_splits = ((dec_seq, dec_seq),)
    gla_p, conv_p, gla_s, conv_s = [], [], [], []
    for l in range(DEPTH):
        lw = (w_in[l], w_alpha_up[l], b_alpha[l], conv_w[l], gla_norm_g[l], w_o[l], ln1_g[l], ln1_b[l],
              w_router[l], b_router[l], w_exp_gate[l], w_exp_up[l], w_exp_down[l],
              w_sh_gate[l], w_sh_up[l], w_sh_down[l], ln2_g[l], ln2_b[l])
        s0 = jnp.zeros((bp, GLA_HEADS, GLA_DK, GLA_DV), state_gla.dtype)
        c0 = jnp.zeros((bp, CONV_K - 1, CONV_WIDTH), state_conv.dtype)
        xp, sp, cp = block(xp, s0, c0, prompt_splits, *lw)
        xs, ss, cs = block(xs, state_gla[l], state_conv[l], sample_splits, *lw)
        gla_p.append(sp)
        conv_p.append(cp.astype(state_conv.dtype))
        gla_s.append(ss)
        conv_s.append(cs.astype(state_conv.dtype))
    y_prompt = xp[:, N_META:]
    y_sample = xs
    return (y_prompt, y_sample, jnp.stack(gla_p), jnp.stack(conv_p), jnp.stack(gla_s), jnp.stack(conv_s))
```

```python
import functools

import jax
import jax.numpy as jnp
from jax import lax
from jax.experimental import pallas as pl
from jax.experimental.pallas import tpu as pltpu

F32 = jnp.float32
BF16 = jnp.bfloat16
HIGHEST = lax.Precision.HIGHEST

D_MODEL = 4096
N_META = 16
GLA_HEADS = 8
GLA_DK = 256
GLA_DV = 512
QK_WIDTH = GLA_HEADS * GLA_DK
GATE_RANK = 16
GATE_TAU = 16.0
GLA_CHUNK = 64
GLA_SUB = 16
CONV_K = 3
N_EXPERTS = 256
TOP_K = 8
N_GROUPS = 8
GROUP_SIZE = N_EXPERTS // N_GROUPS
TOPK_GROUPS = 4
EXPERT_DIM = 1024
ROUTED_SCALE = 2.5
ALPHA = 2.0 ** 0.25
EPS = 1e-5

COL_Q, COL_K, COL_V, COL_G, COL_CB, COL_CC, COL_CH, COL_GA, COL_GC = (
    0, 2048, 4096, 8192, 12288, 16384, 20480, 24576, 28672)
N_MAIN = 32768

LANES = 128
MOE_BLK = 256
F_CHUNK = 256
N_FCH = EXPERT_DIM // F_CHUNK
D_CHUNK = 1024
N_DCH = D_MODEL // D_CHUNK
COMB_TM = 32
VMEM_LIMIT = 60 * 1024 * 1024


def _cparams(sem, vmem=None):
    return pltpu.CompilerParams(dimension_semantics=sem, vmem_limit_bytes=vmem)


def _ln_rows(x, g, b):
    mu = jnp.mean(x, axis=-1, keepdims=True)
    xc = x - mu
    var = jnp.mean(xc * xc, axis=-1, keepdims=True)
    return xc * lax.rsqrt(var + EPS) * g + b


def _sigmoid(x):
    return 1.0 / (1.0 + jnp.exp(-x))


def _silu(x):
    return x * _sigmoid(x)


def _ln_in_kernel(x_ref, g_ref, b_ref, wlr_ref, wau_ref, ba_ref, xn_ref, xnb_ref, la_ref):
    xn = _ln_rows(x_ref[...], g_ref[...], b_ref[...])
    xn_ref[...] = xn
    xb = xn.astype(BF16)
    xnb_ref[...] = xb
    lr = jnp.dot(xb, wlr_ref[...], preferred_element_type=F32)
    z = jnp.dot(lr, wau_ref[...], precision=HIGHEST, preferred_element_type=F32) + ba_ref[...]
    la_ref[...] = (jnp.minimum(z, 0.0) - jnp.log1p(jnp.exp(-jnp.abs(z)))) * (1.0 / GATE_TAU)


def _ln_in(x_all, g, b, wlr, wau, ba, tm=256):
    t = x_all.shape[0]
    return pl.pallas_call(
        _ln_in_kernel,
        grid=(t // tm,),
        in_specs=[
            pl.BlockSpec((tm, D_MODEL), lambda i: (i, 0)),
            pl.BlockSpec((1, D_MODEL), lambda i: (0, 0)),
            pl.BlockSpec((1, D_MODEL), lambda i: (0, 0)),
            pl.BlockSpec((D_MODEL, LANES), lambda i: (0, 0)),
            pl.BlockSpec((LANES, QK_WIDTH), lambda i: (0, 0)),
            pl.BlockSpec((1, QK_WIDTH), lambda i: (0, 0)),
        ],
        out_specs=[
            pl.BlockSpec((tm, D_MODEL), lambda i: (i, 0)),
            pl.BlockSpec((tm, D_MODEL), lambda i: (i, 0)),
            pl.BlockSpec((tm, QK_WIDTH), lambda i: (i, 0)),
        ],
        out_shape=[
            jax.ShapeDtypeStruct((t, D_MODEL), F32),
            jax.ShapeDtypeStruct((t, D_MODEL), BF16),
            jax.ShapeDtypeStruct((t, QK_WIDTH), F32),
        ],
        compiler_params=_cparams(("arbitrary",), VMEM_LIMIT),
        name="ln_in",
    )(x_all, g, b, wlr, wau, ba)


def _mm_kernel(x_ref, w_ref, o_ref):
    o_ref[...] = jnp.dot(x_ref[...], w_ref[...], preferred_element_type=F32)


def _in_proj(xb, w, tm=768, tn=512):
    t, k = xb.shape
    n = w.shape[1]
    return pl.pallas_call(
        _mm_kernel,
        grid=(t // tm, n // tn),
        in_specs=[pl.BlockSpec((tm, k), lambda i, j: (i, 0)),
                  pl.BlockSpec((k, tn), lambda i, j: (0, j))],
        out_specs=pl.BlockSpec((tm, tn), lambda i, j: (i, j)),
        out_shape=jax.ShapeDtypeStruct((t, n), F32),
        compiler_params=_cparams(("arbitrary", "arbitrary"), VMEM_LIMIT),
        name="in_proj",
    )(xb, w)


def _tri(n, strict=False):
    r = lax.broadcasted_iota(jnp.int32, (n, n), 0)
    c = lax.broadcasted_iota(jnp.int32, (n, n), 1)
    return ((r > c) if strict else (r >= c)).astype(F32)


_NT = (((1,), (1,)), ((), ()))
_TN = (((0,), (0,)), ((), ()))


def _gla_prompt_kernel(q_ref, k_ref, v_ref, la_ref, km_ref, vm_ref, lam_ref, gain_ref,
                       o_ref, sout_ref, st_ref, *, n_chunks):
    c = pl.program_id(2)

    @pl.when(c == 0)
    def _():
        bm = jnp.dot(_tri(N_META), lam_ref[...], precision=HIGHEST, preferred_element_type=F32)
        ktm = km_ref[...] * jnp.exp(bm[N_META - 1:N_META, :] - bm)
        st_ref[...] = lax.dot_general(vm_ref[...].astype(BF16), ktm.astype(BF16), _TN,
                                      preferred_element_type=F32)

    cs = GLA_CHUNK
    b = jnp.dot(_tri(cs), la_ref[...], precision=HIGHEST, preferred_element_type=F32)
    q = q_ref[...] * (GLA_DK ** -0.5)
    k = k_ref[...]
    vb = v_ref[...].astype(BF16)
    st = st_ref[...]
    o = lax.dot_general((q * jnp.exp(b)).astype(BF16), st.astype(BF16), _NT, preferred_element_type=F32)

    lane = lax.broadcasted_iota(jnp.int32, (GLA_SUB, cs), 1)
    row = lax.broadcasted_iota(jnp.int32, (GLA_SUB, cs), 0)
    blocks = []
    for blk in range(cs // GLA_SUB):
        r0 = blk * GLA_SUB
        qi = q[r0:r0 + GLA_SUB]
        bi = b[r0:r0 + GLA_SUB]
        if blk > 0:
            bref = b[r0 - 1:r0, :]
            qt = qi * jnp.exp(bi - bref)
            kt = k * jnp.exp(jnp.minimum(bref - b, 0.0))
            s = lax.dot_general(qt.astype(BF16), kt.astype(BF16), _NT, preferred_element_type=F32)
            s = jnp.where(lane < r0, s, 0.0)
        else:
            s = jnp.zeros((GLA_SUB, cs), F32)
        for j in range(GLA_SUB):
            kj = k[r0 + j:r0 + j + 1, :]
            bj = b[r0 + j:r0 + j + 1, :]
            col = jnp.sum(qi * kj * jnp.exp(jnp.minimum(bi - bj, 0.0)), axis=1, keepdims=True)
            s = jnp.where((lane == r0 + j) & (row >= j), col, s)
        blocks.append(s)
    scores = jnp.concatenate(blocks, axis=0)
    o = o + jnp.dot(scores.astype(BF16), vb, preferred_element_type=F32)
    o = o * lax.rsqrt(jnp.mean(o * o, axis=-1, keepdims=True) + EPS) * gain_ref[...]
    o_ref[...] = o

    bl = b[cs - 1:cs, :]
    kt2 = k * jnp.exp(bl - b)
    st_new = st * jnp.exp(bl) + lax.dot_general(vb, kt2.astype(BF16), _TN, preferred_element_type=F32)
    st_ref[...] = st_new

    @pl.when(c == n_chunks - 1)
    def _():
        sout_ref[0, 0] = st_new.T


def _gla_prompt(proj, la, gain, bp, seq, row_meta):
    n_chunks = seq // GLA_CHUNK
    cq, ck, cv = COL_Q // GLA_DK, COL_K // GLA_DK, COL_V // GLA_DV
    mrow = row_meta // N_META
    return pl.pallas_call(
        functools.partial(_gla_prompt_kernel, n_chunks=n_chunks),
        grid=(bp, GLA_HEADS, n_chunks),
        in_specs=[
            pl.BlockSpec((GLA_CHUNK, GLA_DK), lambda b, h, c: (b * n_chunks + c, cq + h)),
            pl.BlockSpec((GLA_CHUNK, GLA_DK), lambda b, h, c: (b * n_chunks + c, ck + h)),
            pl.BlockSpec((GLA_CHUNK, GLA_DV), lambda b, h, c: (b * n_chunks + c, cv + h)),
            pl.BlockSpec((GLA_CHUNK, GLA_DK), lambda b, h, c: (b * n_chunks + c, h)),
            pl.BlockSpec((N_META, GLA_DK), lambda b, h, c: (mrow, ck + h)),
            pl.BlockSpec((N_META, GLA_DV), lambda b, h, c: (mrow, cv + h)),
            pl.BlockSpec((N_META, GLA_DK), lambda b, h, c: (mrow, h)),
            pl.BlockSpec((1, GLA_DV), lambda b, h, c: (0, h)),
        ],
        out_specs=[
            pl.BlockSpec((GLA_CHUNK, GLA_DV), lambda b, h, c: (b * n_chunks + c, h)),
            pl.BlockSpec((1, 1, GLA_DK, GLA_DV), lambda b, h, c: (b, h, 0, 0)),
        ],
        out_shape=[
            jax.ShapeDtypeStruct((bp * seq, D_MODEL), F32),
            jax.ShapeDtypeStruct((bp, GLA_HEADS, GLA_DK, GLA_DV), F32),
        ],
        scratch_shapes=[pltpu.VMEM((GLA_DV, GLA_DK), F32)],
        compiler_params=_cparams(("arbitrary", "arbitrary", "arbitrary")),
        name="gla_prompt",
    )(proj, proj, proj, la, proj, proj, la, gain)


_SAMP_GRP = 8


def _gla_sample_kernel(s_ref, qc_ref, kc_ref, lac_ref, v_ref, gain_ref, so_ref, o_ref):
    qc = qc_ref[0, 0] * (GLA_DK ** -0.5)
    kc = kc_ref[0, 0]
    ac = jnp.exp(lac_ref[0, 0])
    v = v_ref[...]
    rows = []
    for i in range(_SAMP_GRP):
        s_new = ac[:, i:i + 1] * s_ref[i, 0] + kc[:, i:i + 1] * v[i:i + 1, :]
        so_ref[i, 0] = s_new
        rows.append(jnp.sum(qc[:, i:i + 1] * s_new, axis=0, keepdims=True))
    o = jnp.concatenate(rows, axis=0)
    o_ref[...] = o * lax.rsqrt(jnp.mean(o * o, axis=-1, keepdims=True) + EPS) * gain_ref[...]


def _gla_sample(state, qc, kc, lac, proj, gain, bs, row_s):
    ng = bs // _SAMP_GRP
    cv = COL_V // GLA_DV
    r0 = row_s // _SAMP_GRP
    col_spec = pl.BlockSpec((1, 1, GLA_DK, _SAMP_GRP), lambda g, h: (g, h, 0, 0))
    st_spec = pl.BlockSpec((_SAMP_GRP, 1, GLA_DK, GLA_DV), lambda g, h: (g, h, 0, 0))
    return pl.pallas_call(
        _gla_sample_kernel,
        grid=(ng, GLA_HEADS),
        in_specs=[st_spec, col_spec, col_spec, col_spec,
                  pl.BlockSpec((_SAMP_GRP, GLA_DV), lambda g, h: (r0 + g, cv + h)),
                  pl.BlockSpec((1, GLA_DV), lambda g, h: (0, h))],
        out_specs=[st_spec, pl.BlockSpec((_SAMP_GRP, GLA_DV), lambda g, h: (g, h))],
        out_shape=[jax.ShapeDtypeStruct(state.shape, F32),
                   jax.ShapeDtypeStruct((bs, D_MODEL), F32)],
        compiler_params=_cparams(("arbitrary", "arbitrary"), VMEM_LIMIT),
        name="gla_sample",
    )(state, qc, kc, lac, proj, gain)


def _merge(o, g, cb, yconv, ga, gc):
    return _sigmoid(ga) * (o * _silu(g)) + _sigmoid(gc) * (cb * yconv)


def _merge_prompt_kernel(o_ref, g_ref, cb_ref, cc_ref, ch_ref, ga_ref, gc_ref, cw_ref, ccm_ref, chm_ref,
                         m_ref, cs_ref, carry_ref, *, n_rt):
    r = pl.program_id(2)
    rows = cc_ref.shape[0]

    @pl.when(r == 0)
    def _():
        um = ccm_ref[...] * chm_ref[...]
        carry_ref[...] = um[N_META - 8:N_META, :]

    u = cc_ref[...] * ch_ref[...]
    carry = carry_ref[...]
    p1 = carry[7:8, :]
    p2 = carry[6:7, :]
    ri = lax.broadcasted_iota(jnp.int32, u.shape, 0)
    u1 = jnp.where(ri == 0, p1, pltpu.roll(u, 1, 0))
    u2 = jnp.where(ri == 0, p2, jnp.where(ri == 1, p1, pltpu.roll(u, 2, 0)))
    cw = cw_ref[...]
    yconv = cw[0:1, :] * u2 + cw[1:2, :] * u1 + cw[2:3, :] * u
    m_ref[...] = _merge(o_ref[...], g_ref[...], cb_ref[...], yconv, ga_ref[...], gc_ref[...]).astype(BF16)
    tail = u[rows - 8:rows, :]
    carry_ref[...] = tail

    @pl.when(r == n_rt - 1)
    def _():
        cs_ref[...] = tail


def _merge_prompt(o, proj, conv_w, bp, seq, row_meta, rows=1024, cols=512):
    n_rt = seq // rows
    ncb = D_MODEL // cols
    mrow = row_meta // N_META

    def pspec(col0):
        return pl.BlockSpec((rows, cols), lambda b, j, r: (b * n_rt + r, col0 // cols + j))

    def mspec(col0):
        return pl.BlockSpec((N_META, cols), lambda b, j, r: (mrow, col0 // cols + j))

    return pl.pallas_call(
        functools.partial(_merge_prompt_kernel, n_rt=n_rt),
        grid=(bp, ncb, n_rt),
        in_specs=[pspec(0), pspec(COL_G), pspec(COL_CB), pspec(COL_CC), pspec(COL_CH), pspec(COL_GA),
                  pspec(COL_GC), pl.BlockSpec((CONV_K, cols), lambda b, j, r: (0, j)),
                  mspec(COL_CC), mspec(COL_CH)],
        out_specs=[pl.BlockSpec((rows, cols), lambda b, j, r: (b * n_rt + r, j)),
                   pl.BlockSpec((8, cols), lambda b, j, r: (b, j))],
        out_shape=[jax.ShapeDtypeStruct((bp * seq, D_MODEL), BF16),
                   jax.ShapeDtypeStruct((bp * 8, D_MODEL), F32)],
        scratch_shapes=[pltpu.VMEM((8, cols), F32)],
        compiler_params=_cparams(("arbitrary", "arbitrary", "arbitrary"), VMEM_LIMIT),
        name="merge_prompt",
    )(o, proj, proj, proj, proj, proj, proj, conv_w, proj, proj)


def _merge_sample_kernel(o_ref, g_ref, cb_ref, cc_ref, ch_ref, ga_ref, gc_ref, cw_ref, p0_ref, p1_ref,
                         m_ref, u_ref):
    u = cc_ref[...] * ch_ref[...]
    cw = cw_ref[...]
    yconv = cw[0:1, :] * p0_ref[...] + cw[1:2, :] * p1_ref[...] + cw[2:3, :] * u
    m_ref[...] = _merge(o_ref[...], g_ref[...], cb_ref[...], yconv, ga_ref[...], gc_ref[...]).astype(BF16)
    u_ref[...] = u


def _merge_sample(o, proj, conv_w, prefix2d, bs, row_s, cols=512):
    ncb = D_MODEL // cols
    rb = row_s // bs

    def pspec(col0):
        return pl.BlockSpec((bs, cols), lambda j: (rb, col0 // cols + j))

    return pl.pallas_call(
        _merge_sample_kernel,
        grid=(ncb,),
        in_specs=[pl.BlockSpec((bs, cols), lambda j: (0, j)), pspec(COL_G), pspec(COL_CB), pspec(COL_CC),
                  pspec(COL_CH), pspec(COL_GA), pspec(COL_GC),
                  pl.BlockSpec((CONV_K, cols), lambda j: (0, j)),
                  pl.BlockSpec((bs, cols), lambda j: (0, j)),
                  pl.BlockSpec((bs, cols), lambda j: (0, ncb + j))],
        out_specs=[pl.BlockSpec((bs, cols), lambda j: (0, j)),
                   pl.BlockSpec((bs, cols), lambda j: (0, j))],
        out_shape=[jax.ShapeDtypeStruct((bs, D_MODEL), BF16),
                   jax.ShapeDtypeStruct((bs, D_MODEL), F32)],
        compiler_params=_cparams(("arbitrary",)),
        name="merge_sample",
    )(o, proj, proj, proj, proj, proj, proj, conv_w, prefix2d, prefix2d)


def _wo_kernel(m_ref, w_ref, xn_ref, o_ref):
    o_ref[...] = ALPHA * xn_ref[...] + jnp.dot(m_ref[...], w_ref[...], preferred_element_type=F32)


def _out_proj(merged, w_o, xn, tm=640, tn=512):
    t = merged.shape[0]
    return pl.pallas_call(
        _wo_kernel,
        grid=(t // tm, D_MODEL // tn),
        in_specs=[pl.BlockSpec((tm, D_MODEL), lambda i, j: (i, 0)),
                  pl.BlockSpec((D_MODEL, tn), lambda i, j: (0, j)),
                  pl.BlockSpec((tm, tn), lambda i, j: (i, j))],
        out_specs=pl.BlockSpec((tm, tn), lambda i, j: (i, j)),
        out_shape=jax.ShapeDtypeStruct((t, D_MODEL), F32),
        compiler_params=_cparams(("arbitrary", "arbitrary"), VMEM_LIMIT),
        name="out_proj",
    )(merged, w_o, xn)


_RT_TM = 128
_ROUTE_E, _ROUTE_POS, _ROUTE_W = 0, TOP_K, 2 * TOP_K


def _router_kernel(xp_ref, g_ref, b_ref, wr_ref, br_ref, x1_ref, x1b_ref, route_ref, cnt_ref, carry_ref,
                   *, n_tiles):
    i = pl.program_id(0)

    @pl.when(i == 0)
    def _():
        carry_ref[...] = jnp.zeros_like(carry_ref)

    @pl.when(i == n_tiles)
    def _():
        x1_ref[...] = jnp.zeros_like(x1_ref)
        x1b_ref[...] = jnp.zeros_like(x1b_ref)
        route_ref[...] = jnp.zeros_like(route_ref)
        cnt_ref[...] = carry_ref[...]

    @pl.when(i < n_tiles)
    def _():
        tm = _RT_TM
        x1 = _ln_rows(xp_ref[...], g_ref[...], b_ref[...])
        x1_ref[...] = x1
        x1b_ref[...] = x1.astype(BF16)
        logits = jnp.dot(x1, wr_ref[...], precision=HIGHEST, preferred_element_type=F32)
        scores = _sigmoid(logits)
        choice = scores + br_ref[...]
        neg = -jnp.inf
        big = 1e9
        lane_i = lax.broadcasted_iota(jnp.int32, (tm, N_EXPERTS), 1)
        lane = lane_i.astype(F32)
        gid = lane_i // GROUP_SIZE

        gs = []
        for g in range(N_GROUPS):
            cg = jnp.where(gid == g, choice, neg)
            m1 = jnp.max(cg, axis=1, keepdims=True)
            i1 = jnp.min(jnp.where(cg == m1, lane, big), axis=1, keepdims=True)
            m2 = jnp.max(jnp.where(lane == i1, neg, cg), axis=1, keepdims=True)
            gs.append(m1 + m2)
        gsel = jnp.zeros((tm, N_EXPERTS), jnp.bool_)
        for g in range(N_GROUPS):
            rank = jnp.zeros((tm, 1), F32)
            for g2 in range(N_GROUPS):
                if g2 == g:
                    continue
                better = (gs[g2] >= gs[g]) if g2 < g else (gs[g2] > gs[g])
                rank = rank + better.astype(F32)
            gsel = gsel | ((gid == g) & (rank < TOPK_GROUPS))
        masked = jnp.where(gsel, choice, neg)

        sel = jnp.zeros((tm, N_EXPERTS), jnp.bool_)
        onehots, e_k, w_k = [], [], []
        for _ in range(TOP_K):
            m = jnp.max(masked, axis=1, keepdims=True)
            ik = jnp.min(jnp.where(masked == m, lane, big), axis=1, keepdims=True)
            oh = lane == ik
            onehots.append(oh)
            e_k.append(ik)
            w_k.append(jnp.sum(jnp.where(oh, scores, 0.0), axis=1, keepdims=True))
            masked = jnp.where(oh, neg, masked)
            sel = sel | oh
        wsum = w_k[0]
        for kk in range(1, TOP_K):
            wsum = wsum + w_k[kk]

        seld = sel.astype(F32)
        rk = jnp.dot(_tri(tm, strict=True).astype(BF16), seld.astype(BF16),
                     preferred_element_type=F32) + carry_ref[...]
        carry_ref[...] = carry_ref[...] + jnp.sum(seld, axis=0, keepdims=True)

        ol = lax.broadcasted_iota(jnp.int32, (tm, LANES), 1)
        packed = jnp.zeros((tm, LANES), F32)
        for kk in range(TOP_K):
            pos = jnp.sum(jnp.where(onehots[kk], rk, 0.0), axis=1, keepdims=True)
            wgt = w_k[kk] / wsum * ROUTED_SCALE
            packed = jnp.where(ol == _ROUTE_E + kk, e_k[kk], packed)
            packed = jnp.where(ol == _ROUTE_POS + kk, pos, packed)
            packed = jnp.where(ol == _ROUTE_W + kk, wgt, packed)
        route_ref[...] = packed


def _router(xpre, g, b, w_router, b_router):
    t = xpre.shape[0]
    tm = _RT_TM
    n_tiles = t // tm
    last = n_tiles - 1
    return pl.pallas_call(
        functools.partial(_router_kernel, n_tiles=n_tiles),
        grid=(n_tiles + 1,),
        in_specs=[pl.BlockSpec((tm, D_MODEL), lambda i: (jnp.minimum(i, last), 0)),
                  pl.BlockSpec((1, D_MODEL), lambda i: (0, 0)),
                  pl.BlockSpec((1, D_MODEL), lambda i: (0, 0)),
                  pl.BlockSpec((D_MODEL, N_EXPERTS), lambda i: (0, 0)),
                  pl.BlockSpec((1, N_EXPERTS), lambda i: (0, 0))],
        out_specs=[pl.BlockSpec((tm, D_MODEL), lambda i: (i, 0)),
                   pl.BlockSpec((tm, D_MODEL), lambda i: (i, 0)),
                   pl.BlockSpec((tm, LANES), lambda i: (i, 0)),
                   pl.BlockSpec((1, N_EXPERTS), lambda i: (0, 0))],
        out_shape=[jax.ShapeDtypeStruct((t + tm, D_MODEL), F32),
                   jax.ShapeDtypeStruct((t + tm, D_MODEL), BF16),
                   jax.ShapeDtypeStruct((t + tm, LANES), F32),
                   jax.ShapeDtypeStruct((1, N_EXPERTS), F32)],
        scratch_shapes=[pltpu.VMEM((1, N_EXPERTS), F32)],
        compiler_params=_cparams(("arbitrary",), VMEM_LIMIT),
        name="router",
    )(xpre, g, b, w_router, b_router)


def _row_copy(src_hbm, dst, sem, src_row, dst_row):
    return pltpu.make_async_copy(src_hbm.at[pl.ds(src_row, 1)], dst.at[pl.ds(dst_row, 1)], sem)


def _gather_kernel(nvalid_ref, idx_ref, x_hbm, out_ref, buf, sem):
    i = pl.program_id(0)
    n = buf.shape[0]

    @pl.when(i < nvalid_ref[0])
    def _():
        def issue(r, carry):
            _row_copy(x_hbm, buf, sem, idx_ref[0, 0, r], r).start()
            return carry

        lax.fori_loop(0, n, issue, 0)

        def drain(r, carry):
            _row_copy(x_hbm, buf, sem, 0, r).wait()
            return carry

        lax.fori_loop(0, n, drain, 0)
        out_ref[...] = buf[...].astype(BF16)

    @pl.when(i >= nvalid_ref[0])
    def _():
        out_ref[...] = jnp.zeros_like(out_ref)


def _gather_rows(n_valid, slot_tok3, x1, nb_max):
    return pl.pallas_call(
        _gather_kernel,
        grid_spec=pltpu.PrefetchScalarGridSpec(
            num_scalar_prefetch=1,
            grid=(nb_max,),
            in_specs=[pl.BlockSpec((1, 1, MOE_BLK), lambda i, nv: (i, 0, 0), memory_space=pltpu.SMEM),
                      pl.BlockSpec(memory_space=pl.ANY)],
            out_specs=pl.BlockSpec((MOE_BLK, D_MODEL), lambda i, nv: (i, 0)),
            scratch_shapes=[pltpu.VMEM((MOE_BLK, D_MODEL), F32), pltpu.SemaphoreType.DMA(())],
        ),
        out_shape=jax.ShapeDtypeStruct((nb_max * MOE_BLK, D_MODEL), BF16),
        compiler_params=_cparams(("arbitrary",), VMEM_LIMIT),
        name="gather_rows",
    )(n_valid, slot_tok3, x1)


def _experts_kernel(n_items_ref, it_s_ref, it_r_ref, it_x_ref, it_e_ref, it_f_ref, it_d_ref, it_ob_ref,
                    it_oc_ref, x_ref, wg_ref, wu_ref, wd_ref, o_ref, h_ref, wgb_ref, wub_ref, wdb_ref):
    i = pl.program_id(0)
    s = it_s_ref[i]
    r = it_r_ref[i]
    valid = i < n_items_ref[0]
    row0 = pl.multiple_of(r * MOE_BLK, MOE_BLK)

    @pl.when(valid & (s < N_FCH) & (r == 0))
    def _():
        wgb_ref[...] = wg_ref[0].astype(BF16)
        wub_ref[...] = wu_ref[0].astype(BF16)

    @pl.when(valid & (s < N_FCH))
    def _():
        x = x_ref[...]
        a = jnp.dot(x, wgb_ref[...], preferred_element_type=F32)
        u = jnp.dot(x, wub_ref[...], preferred_element_type=F32)
        h_ref[s, pl.ds(row0, MOE_BLK), :] = (_silu(a) * u).astype(BF16)

    @pl.when(valid & (s >= N_FCH) & (r == 0))
    def _():
        wdb_ref[...] = wd_ref[0].astype(BF16)

    @pl.when(valid & (s >= N_FCH))
    def _():
        acc = jnp.zeros((MOE_BLK, D_CHUNK), F32)
        for f in range(N_FCH):
            acc = acc + jnp.dot(h_ref[f, pl.ds(row0, MOE_BLK), :], wdb_ref[f * F_CHUNK:(f + 1) * F_CHUNK, :],
                                preferred_element_type=F32)
        o_ref[...] = acc

    @pl.when(jnp.logical_not(valid))
    def _():
        o_ref[...] = jnp.zeros_like(o_ref)


def _experts(items, x_sorted, w_gate, w_up, w_down, n_items_max, h_rows):
    n_rows = x_sorted.shape[0]

    def im_x(i, n, s, r, x, e, f, d, ob, oc):
        return (x[i], 0)

    def im_wgu(i, n, s, r, x, e, f, d, ob, oc):
        return (e[i], 0, f[i])

    def im_wd(i, n, s, r, x, e, f, d, ob, oc):
        return (e[i], 0, d[i])

    def im_o(i, n, s, r, x, e, f, d, ob, oc):
        return (ob[i], oc[i])

    return pl.pallas_call(
        _experts_kernel,
        grid_spec=pltpu.PrefetchScalarGridSpec(
            num_scalar_prefetch=9,
            grid=(n_items_max,),
            in_specs=[pl.BlockSpec((MOE_BLK, D_MODEL), im_x),
                      pl.BlockSpec((1, D_MODEL, F_CHUNK), im_wgu),
                      pl.BlockSpec((1, D_MODEL, F_CHUNK), im_wgu),
                      pl.BlockSpec((1, EXPERT_DIM, D_CHUNK), im_wd)],
            out_specs=pl.BlockSpec((MOE_BLK, D_CHUNK), im_o),
            scratch_shapes=[pltpu.VMEM((N_FCH, h_rows, F_CHUNK), BF16),
                            pltpu.VMEM((D_MODEL, F_CHUNK), BF16),
                            pltpu.VMEM((D_MODEL, F_CHUNK), BF16),
                            pltpu.VMEM((EXPERT_DIM, D_CHUNK), BF16)],
        ),
        out_shape=jax.ShapeDtypeStruct((n_rows, D_MODEL), F32),
        compiler_params=_cparams(("arbitrary",), VMEM_LIMIT),
        name="experts",
    )(items["n"], items["s"], items["r"], items["x"], items["e"], items["f"], items["d"], items["ob"],
      items["oc"], x_sorted, w_gate, w_up, w_down)


def _expert_items(nb, n_items_max):
    n_steps = N_FCH + N_DCH
    blk_start = jnp.cumsum(nb) - nb
    item_end = n_steps * (blk_start + nb)
    n_items = n_steps * jnp.sum(nb)
    i = jnp.minimum(jnp.arange(n_items_max, dtype=jnp.int32), n_items - 1)
    e = jnp.minimum(jnp.searchsorted(item_end, i, side="right").astype(jnp.int32), nb.shape[0] - 1)
    nbe = jnp.maximum(nb[e], 1)
    loc = i - n_steps * blk_start[e]
    s = loc // nbe
    r = loc % nbe
    stage1 = s < N_FCH
    rb = blk_start[e] + r
    nb_max = n_items_max // n_steps
    spare = jnp.arange(n_items_max, dtype=jnp.int32) - n_items
    spare_blk = jnp.sum(nb) + spare // N_DCH
    is_spare = spare >= 0
    spare_ok = spare_blk < nb_max
    ob = jnp.where(stage1, blk_start[e], rb)
    oc = jnp.where(stage1, 0, s - N_FCH)
    ob = jnp.where(is_spare, jnp.where(spare_ok, spare_blk, nb_max - 1), ob)
    oc = jnp.where(is_spare, jnp.where(spare_ok, spare % N_DCH, N_DCH - 1), oc)
    return {
        "n": n_items.reshape(1).astype(jnp.int32),
        "s": s, "r": r, "e": e,
        "x": jnp.where(stage1, rb, blk_start[e] + nbe - 1),
        "f": jnp.where(stage1, s, N_FCH - 1),
        "d": jnp.where(stage1, 0, s - N_FCH),
        "ob": ob, "oc": oc,
    }


def _combine_kernel(slot_ref, w_ref, ysh_ref, x1_ref, g_ref, b_ref, y_hbm, out_ref, buf, sem):
    n = buf.shape[0]
    tm = COMB_TM

    def issue(j, carry):
        _row_copy(y_hbm, buf, sem, slot_ref[0, 0, j], j).start()
        return carry

    lax.fori_loop(0, n, issue, 0)

    def drain(j, carry):
        _row_copy(y_hbm, buf, sem, 0, j).wait()
        return carry

    lax.fori_loop(0, n, drain, 0)
    w = w_ref[...]
    acc = w[:, 0:1] * buf[0:tm, :]
    for kk in range(1, TOP_K):
        acc = acc + w[:, kk:kk + 1] * buf[kk * tm:(kk + 1) * tm, :]
    f = acc + ysh_ref[...]
    out_ref[...] = _ln_rows(ALPHA * x1_ref[...] + f, g_ref[...], b_ref[...])


def _combine(slots3, w, ysh, x1, g, b, y_slots, t):
    tm = COMB_TM
    return pl.pallas_call(
        _combine_kernel,
        grid=(t // tm,),
        in_specs=[pl.BlockSpec((1, 1, tm * TOP_K), lambda i: (i, 0, 0), memory_space=pltpu.SMEM),
                  pl.BlockSpec((tm, TOP_K), lambda i: (i, 0)),
                  pl.BlockSpec((tm, D_MODEL), lambda i: (i, 0)),
                  pl.BlockSpec((tm, D_MODEL), lambda i: (i, 0)),
                  pl.BlockSpec((1, D_MODEL), lambda i: (0, 0)),
                  pl.BlockSpec((1, D_MODEL), lambda i: (0, 0)),
                  pl.BlockSpec(memory_space=pl.ANY)],
        out_specs=pl.BlockSpec((tm, D_MODEL), lambda i: (i, 0)),
        out_shape=jax.ShapeDtypeStruct((t, D_MODEL), F32),
        scratch_shapes=[pltpu.VMEM((tm * TOP_K, D_MODEL), F32), pltpu.SemaphoreType.DMA(())],
        compiler_params=_cparams(("arbitrary",), VMEM_LIMIT),
        name="combine",
    )(slots3, w, ysh, x1, g, b, y_slots)


def kernel(x_prompt, x_sample, state_gla, state_conv, meta_tokens, ln_emb_g, ln_emb_b, w_in, w_alpha_up, b_alpha, conv_w, gla_norm_g, w_o, ln1_g, ln1_b, w_router, b_router, w_exp_gate, w_exp_up, w_exp_down, w_sh_gate, w_sh_up, w_sh_down, ln2_g, ln2_b):
    bp, seq, d = x_prompt.shape
    bs = x_sample.shape[0]
    assert d == D_MODEL and x_sample.shape[1] == 1 and w_in.shape[0] == 1
    tp = bp * seq
    row_s = tp
    row_m = tp + bs
    t_tok = tp + bs
    t_all = -(-(row_m + N_META) // 256) * 256
    assert seq % 1024 == 0 and bs == 128 and t_tok % 640 == 0 and t_all % 768 == 0

    row = lambda a: a.reshape(1, -1)

    x_all = jnp.concatenate([x_prompt.reshape(tp, d), x_sample.reshape(bs, d), meta_tokens.astype(F32),
                             jnp.zeros((t_all - row_m - N_META, d), F32)], axis=0)
    lr0 = COL_V + GLA_HEADS * GLA_DV
    w_in0 = w_in[0]
    w_main = jnp.concatenate([w_in0[:, :lr0], w_in0[:, lr0 + GATE_RANK:]], axis=1).astype(BF16)
    w_lr = jnp.pad(w_in0[:, lr0:lr0 + GATE_RANK], ((0, 0), (0, LANES - GATE_RANK))).astype(BF16)
    w_au = jnp.pad(w_alpha_up[0], ((0, LANES - GATE_RANK), (0, 0)))
    xn, xnb, la = _ln_in(x_all, row(ln_emb_g), row(ln_emb_b), w_lr, w_au, row(b_alpha[0]))
    proj = _in_proj(xnb, w_main)

    gain = row(gla_norm_g[0])
    o_p, gla_p = _gla_prompt(proj, la, gain, bp, seq, row_m)

    def cols(a):
        return a.reshape(bs // _SAMP_GRP, _SAMP_GRP, GLA_HEADS, GLA_DK).transpose(0, 2, 3, 1)

    ps = proj[row_s:row_s + bs]
    gla_s, o_s = _gla_sample(state_gla[0], cols(ps[:, COL_Q:COL_Q + QK_WIDTH]),
                             cols(ps[:, COL_K:COL_K + QK_WIDTH]), cols(la[row_s:row_s + bs]),
                             proj, gain, bs, row_s)

    cw = conv_w[0]
    m_p, cs_p = _merge_prompt(o_p, proj, cw, bp, seq, row_m)
    m_s, u_s = _merge_sample(o_s, proj, cw, state_conv[0].reshape(bs, (CONV_K - 1) * d), bs, row_s)
    conv_p = cs_p.reshape(bp, 8, d)[:, 8 - (CONV_K - 1):, :]
    conv_s = jnp.stack([state_conv[0][:, 1, :], u_s], axis=1)

    merged = jnp.concatenate([m_p, m_s], axis=0)
    xpre = _out_proj(merged, w_o[0].astype(BF16), xn)
    x1, x1b, route, counts = _router(xpre, row(ln1_g[0]), row(ln1_b[0]), w_router[0], row(b_router[0]))

    n_assign = t_tok * TOP_K
    nb_max = (n_assign + N_EXPERTS * (MOE_BLK - 1)) // MOE_BLK
    eidx = route[:t_tok, _ROUTE_E:_ROUTE_E + TOP_K].astype(jnp.int32)
    pos = route[:t_tok, _ROUTE_POS:_ROUTE_POS + TOP_K].astype(jnp.int32)
    wts = route[:t_tok, _ROUTE_W:_ROUTE_W + TOP_K]
    cnt = counts[0].astype(jnp.int32)
    nb = (cnt + MOE_BLK - 1) // MOE_BLK
    blk_start = jnp.cumsum(nb) - nb
    slot = blk_start[eidx] * MOE_BLK + pos
    tok = jnp.broadcast_to(jnp.arange(t_tok, dtype=jnp.int32)[:, None], (t_tok, TOP_K))
    slot_tok = jnp.full((nb_max * MOE_BLK,), t_tok, jnp.int32).at[slot.reshape(-1)].set(tok.reshape(-1))
    n_blocks = jnp.sum(nb).reshape(1).astype(jnp.int32)
    x_sorted = _gather_rows(n_blocks, slot_tok.reshape(nb_max, 1, MOE_BLK), x1, nb_max)

    n_steps = N_FCH + N_DCH
    h_rows = -(-t_tok // MOE_BLK) * MOE_BLK
    y_slots = _experts(_expert_items(nb, n_steps * nb_max), x_sorted, w_exp_gate[0], w_exp_up[0],
                       w_exp_down[0], n_steps * nb_max, h_rows)
    nb_sh = x1b.shape[0] // MOE_BLK
    ysh = _experts(_expert_items(jnp.full((1,), nb_sh, jnp.int32), n_steps * nb_sh), x1b, w_sh_gate,
                   w_sh_up, w_sh_down, n_steps * nb_sh, nb_sh * MOE_BLK)

    slots3 = slot.reshape(t_tok // COMB_TM, COMB_TM, TOP_K).transpose(0, 2, 1).reshape(
        t_tok // COMB_TM, 1, COMB_TM * TOP_K)
    y = _combine(slots3, wts, ysh, x1, row(ln2_g[0]), row(ln2_b[0]), y_slots, t_tok)

    y_prompt = y[:tp].reshape(bp, seq, d)
    y_sample = y[tp:].reshape(bs, 1, d)
    return (y_prompt, y_sample, gla_p[None], conv_p[None], gla_s[None], conv_s[None])
```

```python
import functools

import jax
import jax.numpy as jnp
from jax import lax
from jax.experimental import pallas as pl
from jax.experimental.pallas import tpu as pltpu

F32 = jnp.float32
BF16 = jnp.bfloat16
HIGHEST = lax.Precision.HIGHEST

D_MODEL = 4096
N_META = 16
GLA_HEADS = 8
GLA_DK = 256
GLA_DV = 512
QK_WIDTH = GLA_HEADS * GLA_DK
GATE_RANK = 16
GATE_TAU = 16.0
GLA_CHUNK = 64
GLA_SUB = 16
CONV_K = 3
N_EXPERTS = 256
TOP_K = 8
N_GROUPS = 8
GROUP_SIZE = N_EXPERTS // N_GROUPS
TOPK_GROUPS = 4
EXPERT_DIM = 1024
ROUTED_SCALE = 2.5
ALPHA = 2.0 ** 0.25
EPS = 1e-5

COL_Q, COL_K, COL_V, COL_G, COL_CB, COL_CC, COL_CH, COL_GA, COL_GC = (
    0, 2048, 4096, 8192, 12288, 16384, 20480, 24576, 28672)
N_MAIN = 32768

LANES = 128
SEG = 512
YBLK = 256
F_CHUNK = 256
N_FCH = EXPERT_DIM // F_CHUNK
N_ACH = 2 * N_FCH
D_CHUNK = 1024
N_DCH = D_MODEL // D_CHUNK
A_SLOTS = 4
B_SLOTS = 2
ZFILL_PER_STEP = 4
COMB_TM = 64
VMEM_LIMIT = 60 * 1024 * 1024


def _cparams(sem, vmem=None):
    return pltpu.CompilerParams(dimension_semantics=sem, vmem_limit_bytes=vmem)


def _ln_rows(x, g, b):
    mu = jnp.mean(x, axis=-1, keepdims=True)
    xc = x - mu
    var = jnp.mean(xc * xc, axis=-1, keepdims=True)
    return xc * lax.rsqrt(var + EPS) * g + b


def _sigmoid(x):
    return 1.0 / (1.0 + jnp.exp(-x))


def _silu(x):
    return x * _sigmoid(x)


def _ln_in_kernel(x_ref, g_ref, b_ref, wlr_ref, wau_ref, ba_ref, xn_ref, xnb_ref, la_ref):
    xn = _ln_rows(x_ref[...], g_ref[...], b_ref[...])
    xn_ref[...] = xn
    xb = xn.astype(BF16)
    xnb_ref[...] = xb
    lr = jnp.dot(xb, wlr_ref[...], preferred_element_type=F32)
    z = jnp.dot(lr, wau_ref[...], precision=HIGHEST, preferred_element_type=F32) + ba_ref[...]
    la_ref[...] = (jnp.minimum(z, 0.0) - jnp.log1p(jnp.exp(-jnp.abs(z)))) * (1.0 / GATE_TAU)


def _ln_in(x_all, g, b, wlr, wau, ba, tm=256):
    t = x_all.shape[0]
    return pl.pallas_call(
        _ln_in_kernel,
        grid=(t // tm,),
        in_specs=[
            pl.BlockSpec((tm, D_MODEL), lambda i: (i, 0)),
            pl.BlockSpec((1, D_MODEL), lambda i: (0, 0)),
            pl.BlockSpec((1, D_MODEL), lambda i: (0, 0)),
            pl.BlockSpec((D_MODEL, LANES), lambda i: (0, 0)),
            pl.BlockSpec((LANES, QK_WIDTH), lambda i: (0, 0)),
            pl.BlockSpec((1, QK_WIDTH), lambda i: (0, 0)),
        ],
        out_specs=[
            pl.BlockSpec((tm, D_MODEL), lambda i: (i, 0)),
            pl.BlockSpec((tm, D_MODEL), lambda i: (i, 0)),
            pl.BlockSpec((tm, QK_WIDTH), lambda i: (i, 0)),
        ],
        out_shape=[
            jax.ShapeDtypeStruct((t, D_MODEL), F32),
            jax.ShapeDtypeStruct((t, D_MODEL), BF16),
            jax.ShapeDtypeStruct((t, QK_WIDTH), F32),
        ],
        compiler_params=_cparams(("arbitrary",), VMEM_LIMIT),
        name="ln_in",
    )(x_all, g, b, wlr, wau, ba)


def _mm_kernel(x_ref, w_ref, o_ref):
    o_ref[...] = jnp.dot(x_ref[...], w_ref[...], preferred_element_type=F32)


def _in_proj(xb, w, tm=768, tn=512):
    t, k = xb.shape
    n = w.shape[1]
    return pl.pallas_call(
        _mm_kernel,
        grid=(t // tm, n // tn),
        in_specs=[pl.BlockSpec((tm, k), lambda i, j: (i, 0)),
                  pl.BlockSpec((k, tn), lambda i, j: (0, j))],
        out_specs=pl.BlockSpec((tm, tn), lambda i, j: (i, j)),
        out_shape=jax.ShapeDtypeStruct((t, n), F32),
        compiler_params=_cparams(("arbitrary", "arbitrary"), VMEM_LIMIT),
        name="in_proj",
    )(xb, w)


def _tri(n, strict=False):
    r = lax.broadcasted_iota(jnp.int32, (n, n), 0)
    c = lax.broadcasted_iota(jnp.int32, (n, n), 1)
    return ((r > c) if strict else (r >= c)).astype(F32)


_NT = (((1,), (1,)), ((), ()))
_TN = (((0,), (0,)), ((), ()))


def _gla_prompt_kernel(q_ref, k_ref, v_ref, la_ref, km_ref, vm_ref, lam_ref, gain_ref,
                       o_ref, sout_ref, st_ref, *, n_chunks):
    c = pl.program_id(2)

    @pl.when(c == 0)
    def _():
        bm = jnp.dot(_tri(N_META), lam_ref[...], precision=HIGHEST, preferred_element_type=F32)
        ktm = km_ref[...] * jnp.exp(bm[N_META - 1:N_META, :] - bm)
        st_ref[...] = lax.dot_general(vm_ref[...].astype(BF16), ktm.astype(BF16), _TN,
                                      preferred_element_type=F32)

    cs = GLA_CHUNK
    b = jnp.dot(_tri(cs), la_ref[...], precision=HIGHEST, preferred_element_type=F32)
    q = q_ref[...] * (GLA_DK ** -0.5)
    k = k_ref[...]
    vb = v_ref[...].astype(BF16)
    st = st_ref[...]
    o = lax.dot_general((q * jnp.exp(b)).astype(BF16), st.astype(BF16), _NT, preferred_element_type=F32)

    lane = lax.broadcasted_iota(jnp.int32, (GLA_SUB, cs), 1)
    row = lax.broadcasted_iota(jnp.int32, (GLA_SUB, cs), 0)
    blocks = []
    for blk in range(cs // GLA_SUB):
        r0 = blk * GLA_SUB
        qi = q[r0:r0 + GLA_SUB]
        bi = b[r0:r0 + GLA_SUB]
        if blk > 0:
            bref = b[r0 - 1:r0, :]
            qt = qi * jnp.exp(bi - bref)
            kt = k * jnp.exp(jnp.minimum(bref - b, 0.0))
            s = lax.dot_general(qt.astype(BF16), kt.astype(BF16), _NT, preferred_element_type=F32)
            s = jnp.where(lane < r0, s, 0.0)
        else:
            s = jnp.zeros((GLA_SUB, cs), F32)
        for j in range(GLA_SUB):
            kj = k[r0 + j:r0 + j + 1, :]
            bj = b[r0 + j:r0 + j + 1, :]
            col = jnp.sum(qi * kj * jnp.exp(jnp.minimum(bi - bj, 0.0)), axis=1, keepdims=True)
            s = jnp.where((lane == r0 + j) & (row >= j), col, s)
        blocks.append(s)
    scores = jnp.concatenate(blocks, axis=0)
    o = o + jnp.dot(scores.astype(BF16), vb, preferred_element_type=F32)
    o = o * lax.rsqrt(jnp.mean(o * o, axis=-1, keepdims=True) + EPS) * gain_ref[...]
    o_ref[...] = o

    bl = b[cs - 1:cs, :]
    kt2 = k * jnp.exp(bl - b)
    st_new = st * jnp.exp(bl) + lax.dot_general(vb, kt2.astype(BF16), _TN, preferred_element_type=F32)
    st_ref[...] = st_new

    @pl.when(c == n_chunks - 1)
    def _():
        sout_ref[0, 0] = st_new.T


def _gla_prompt(proj, la, gain, bp, seq, row_meta):
    n_chunks = seq // GLA_CHUNK
    cq, ck, cv = COL_Q // GLA_DK, COL_K // GLA_DK, COL_V // GLA_DV
    mrow = row_meta // N_META
    return pl.pallas_call(
        functools.partial(_gla_prompt_kernel, n_chunks=n_chunks),
        grid=(bp, GLA_HEADS, n_chunks),
        in_specs=[
            pl.BlockSpec((GLA_CHUNK, GLA_DK), lambda b, h, c: (b * n_chunks + c, cq + h)),
            pl.BlockSpec((GLA_CHUNK, GLA_DK), lambda b, h, c: (b * n_chunks + c, ck + h)),
            pl.BlockSpec((GLA_CHUNK, GLA_DV), lambda b, h, c: (b * n_chunks + c, cv + h)),
            pl.BlockSpec((GLA_CHUNK, GLA_DK), lambda b, h, c: (b * n_chunks + c, h)),
            pl.BlockSpec((N_META, GLA_DK), lambda b, h, c: (mrow, ck + h)),
            pl.BlockSpec((N_META, GLA_DV), lambda b, h, c: (mrow, cv + h)),
            pl.BlockSpec((N_META, GLA_DK), lambda b, h, c: (mrow, h)),
            pl.BlockSpec((1, GLA_DV), lambda b, h, c: (0, h)),
        ],
        out_specs=[
            pl.BlockSpec((GLA_CHUNK, GLA_DV), lambda b, h, c: (b * n_chunks + c, h)),
            pl.BlockSpec((1, 1, GLA_DK, GLA_DV), lambda b, h, c: (b, h, 0, 0)),
        ],
        out_shape=[
            jax.ShapeDtypeStruct((bp * seq, D_MODEL), F32),
            jax.ShapeDtypeStruct((bp, GLA_HEADS, GLA_DK, GLA_DV), F32),
        ],
        scratch_shapes=[pltpu.VMEM((GLA_DV, GLA_DK), F32)],
        compiler_params=_cparams(("arbitrary", "arbitrary", "arbitrary")),
        name="gla_prompt",
    )(proj, proj, proj, la, proj, proj, la, gain)


_SAMP_GRP = 8


def _gla_sample_kernel(s_ref, qc_ref, kc_ref, lac_ref, v_ref, gain_ref, so_ref, o_ref):
    qc = qc_ref[0, 0] * (GLA_DK ** -0.5)
    kc = kc_ref[0, 0]
    ac = jnp.exp(lac_ref[0, 0])
    v = v_ref[...]
    rows = []
    for i in range(_SAMP_GRP):
        s_new = ac[:, i:i + 1] * s_ref[i, 0] + kc[:, i:i + 1] * v[i:i + 1, :]
        so_ref[i, 0] = s_new
        rows.append(jnp.sum(qc[:, i:i + 1] * s_new, axis=0, keepdims=True))
    o = jnp.concatenate(rows, axis=0)
    o_ref[...] = o * lax.rsqrt(jnp.mean(o * o, axis=-1, keepdims=True) + EPS) * gain_ref[...]


def _gla_sample(state, qc, kc, lac, proj, gain, bs, row_s):
    ng = bs // _SAMP_GRP
    cv = COL_V // GLA_DV
    r0 = row_s // _SAMP_GRP
    col_spec = pl.BlockSpec((1, 1, GLA_DK, _SAMP_GRP), lambda g, h: (g, h, 0, 0))
    st_spec = pl.BlockSpec((_SAMP_GRP, 1, GLA_DK, GLA_DV), lambda g, h: (g, h, 0, 0))
    return pl.pallas_call(
        _gla_sample_kernel,
        grid=(ng, GLA_HEADS),
        in_specs=[st_spec, col_spec, col_spec, col_spec,
                  pl.BlockSpec((_SAMP_GRP, GLA_DV), lambda g, h: (r0 + g, cv + h)),
                  pl.BlockSpec((1, GLA_DV), lambda g, h: (0, h))],
        out_specs=[st_spec, pl.BlockSpec((_SAMP_GRP, GLA_DV), lambda g, h: (g, h))],
        out_shape=[jax.ShapeDtypeStruct(state.shape, F32),
                   jax.ShapeDtypeStruct((bs, D_MODEL), F32)],
        compiler_params=_cparams(("arbitrary", "arbitrary"), VMEM_LIMIT),
        name="gla_sample",
    )(state, qc, kc, lac, proj, gain)


def _merge(o, g, cb, yconv, ga, gc):
    return _sigmoid(ga) * (o * _silu(g)) + _sigmoid(gc) * (cb * yconv)


def _merge_prompt_kernel(o_ref, g_ref, cb_ref, cc_ref, ch_ref, ga_ref, gc_ref, cw_ref, ccm_ref, chm_ref,
                         m_ref, cs_ref, carry_ref, *, n_rt):
    r = pl.program_id(2)
    rows = cc_ref.shape[0]

    @pl.when(r == 0)
    def _():
        um = ccm_ref[...] * chm_ref[...]
        carry_ref[...] = um[N_META - 8:N_META, :]

    u = cc_ref[...] * ch_ref[...]
    carry = carry_ref[...]
    p1 = carry[7:8, :]
    p2 = carry[6:7, :]
    ri = lax.broadcasted_iota(jnp.int32, u.shape, 0)
    u1 = jnp.where(ri == 0, p1, pltpu.roll(u, 1, 0))
    u2 = jnp.where(ri == 0, p2, jnp.where(ri == 1, p1, pltpu.roll(u, 2, 0)))
    cw = cw_ref[...]
    yconv = cw[0:1, :] * u2 + cw[1:2, :] * u1 + cw[2:3, :] * u
    m_ref[...] = _merge(o_ref[...], g_ref[...], cb_ref[...], yconv, ga_ref[...], gc_ref[...]).astype(BF16)
    tail = u[rows - 8:rows, :]
    carry_ref[...] = tail

    @pl.when(r == n_rt - 1)
    def _():
        cs_ref[...] = tail


def _merge_prompt(o, proj, conv_w, bp, seq, row_meta, rows=1024, cols=512):
    n_rt = seq // rows
    ncb = D_MODEL // cols
    mrow = row_meta // N_META

    def pspec(col0):
        return pl.BlockSpec((rows, cols), lambda b, j, r: (b * n_rt + r, col0 // cols + j))

    def mspec(col0):
        return pl.BlockSpec((N_META, cols), lambda b, j, r: (mrow, col0 // cols + j))

    return pl.pallas_call(
        functools.partial(_merge_prompt_kernel, n_rt=n_rt),
        grid=(bp, ncb, n_rt),
        in_specs=[pspec(0), pspec(COL_G), pspec(COL_CB), pspec(COL_CC), pspec(COL_CH), pspec(COL_GA),
                  pspec(COL_GC), pl.BlockSpec((CONV_K, cols), lambda b, j, r: (0, j)),
                  mspec(COL_CC), mspec(COL_CH)],
        out_specs=[pl.BlockSpec((rows, cols), lambda b, j, r: (b * n_rt + r, j)),
                   pl.BlockSpec((8, cols), lambda b, j, r: (b, j))],
        out_shape=[jax.ShapeDtypeStruct((bp * seq, D_MODEL), BF16),
                   jax.ShapeDtypeStruct((bp * 8, D_MODEL), F32)],
        scratch_shapes=[pltpu.VMEM((8, cols), F32)],
        compiler_params=_cparams(("arbitrary", "arbitrary", "arbitrary"), VMEM_LIMIT),
        name="merge_prompt",
    )(o, proj, proj, proj, proj, proj, proj, conv_w, proj, proj)


def _merge_sample_kernel(o_ref, g_ref, cb_ref, cc_ref, ch_ref, ga_ref, gc_ref, cw_ref, p0_ref, p1_ref,
                         m_ref, u_ref):
    u = cc_ref[...] * ch_ref[...]
    cw = cw_ref[...]
    yconv = cw[0:1, :] * p0_ref[...] + cw[1:2, :] * p1_ref[...] + cw[2:3, :] * u
    m_ref[...] = _merge(o_ref[...], g_ref[...], cb_ref[...], yconv, ga_ref[...], gc_ref[...]).astype(BF16)
    u_ref[...] = u


def _merge_sample(o, proj, conv_w, prefix2d, bs, row_s, cols=512):
    ncb = D_MODEL // cols
    rb = row_s // bs

    def pspec(col0):
        return pl.BlockSpec((bs, cols), lambda j: (rb, col0 // cols + j))

    return pl.pallas_call(
        _merge_sample_kernel,
        grid=(ncb,),
        in_specs=[pl.BlockSpec((bs, cols), lambda j: (0, j)), pspec(COL_G), pspec(COL_CB), pspec(COL_CC),
                  pspec(COL_CH), pspec(COL_GA), pspec(COL_GC),
                  pl.BlockSpec((CONV_K, cols), lambda j: (0, j)),
                  pl.BlockSpec((bs, cols), lambda j: (0, j)),
                  pl.BlockSpec((bs, cols), lambda j: (0, ncb + j))],
        out_specs=[pl.BlockSpec((bs, cols), lambda j: (0, j)),
                   pl.BlockSpec((bs, cols), lambda j: (0, j))],
        out_shape=[jax.ShapeDtypeStruct((bs, D_MODEL), BF16),
                   jax.ShapeDtypeStruct((bs, D_MODEL), F32)],
        compiler_params=_cparams(("arbitrary",)),
        name="merge_sample",
    )(o, proj, proj, proj, proj, proj, proj, conv_w, prefix2d, prefix2d)


def _wo_kernel(m_ref, w_ref, xn_ref, o_ref):
    o_ref[...] = ALPHA * xn_ref[...] + jnp.dot(m_ref[...], w_ref[...], preferred_element_type=F32)


def _out_proj(merged, w_o, xn, tm=640, tn=512):
    t = merged.shape[0]
    return pl.pallas_call(
        _wo_kernel,
        grid=(t // tm, D_MODEL // tn),
        in_specs=[pl.BlockSpec((tm, D_MODEL), lambda i, j: (i, 0)),
                  pl.BlockSpec((D_MODEL, tn), lambda i, j: (0, j)),
                  pl.BlockSpec((tm, tn), lambda i, j: (i, j))],
        out_specs=pl.BlockSpec((tm, tn), lambda i, j: (i, j)),
        out_shape=jax.ShapeDtypeStruct((t, D_MODEL), F32),
        compiler_params=_cparams(("arbitrary", "arbitrary"), VMEM_LIMIT),
        name="out_proj",
    )(merged, w_o, xn)


_RT_TM = 128
_ROUTE_E, _ROUTE_POS, _ROUTE_W = 0, TOP_K, 2 * TOP_K


def _router_kernel(xp_ref, g_ref, b_ref, wr_ref, br_ref, x1_ref, x1b_ref, route_ref, cnt_ref, carry_ref,
                   *, n_tiles):
    i = pl.program_id(0)

    @pl.when(i == 0)
    def _():
        carry_ref[...] = jnp.zeros_like(carry_ref)

    @pl.when(i >= n_tiles)
    def _():
        x1_ref[...] = jnp.zeros_like(x1_ref)
        x1b_ref[...] = jnp.zeros_like(x1b_ref)
        route_ref[...] = jnp.zeros_like(route_ref)

    @pl.when(i == n_tiles)
    def _():
        cnt_ref[...] = carry_ref[...]

    @pl.when(i < n_tiles)
    def _():
        tm = _RT_TM
        x1 = _ln_rows(xp_ref[...], g_ref[...], b_ref[...])
        x1_ref[...] = x1
        x1b_ref[...] = x1.astype(BF16)
        logits = jnp.dot(x1, wr_ref[...], precision=HIGHEST, preferred_element_type=F32)
        scores = _sigmoid(logits)
        choice = scores + br_ref[...]
        neg = -jnp.inf
        big = 1e9
        lane_i = lax.broadcasted_iota(jnp.int32, (tm, N_EXPERTS), 1)
        lane = lane_i.astype(F32)
        gid = lane_i // GROUP_SIZE

        gs = []
        for g in range(N_GROUPS):
            cg = jnp.where(gid == g, choice, neg)
            m1 = jnp.max(cg, axis=1, keepdims=True)
            i1 = jnp.min(jnp.where(cg == m1, lane, big), axis=1, keepdims=True)
            m2 = jnp.max(jnp.where(lane == i1, neg, cg), axis=1, keepdims=True)
            gs.append(m1 + m2)
        gsel = jnp.zeros((tm, N_EXPERTS), jnp.bool_)
        for g in range(N_GROUPS):
            rank = jnp.zeros((tm, 1), F32)
            for g2 in range(N_GROUPS):
                if g2 == g:
                    continue
                better = (gs[g2] >= gs[g]) if g2 < g else (gs[g2] > gs[g])
                rank = rank + better.astype(F32)
            gsel = gsel | ((gid == g) & (rank < TOPK_GROUPS))
        masked = jnp.where(gsel, choice, neg)

        sel = jnp.zeros((tm, N_EXPERTS), jnp.bool_)
        onehots, e_k, w_k = [], [], []
        for _ in range(TOP_K):
            m = jnp.max(masked, axis=1, keepdims=True)
            ik = jnp.min(jnp.where(masked == m, lane, big), axis=1, keepdims=True)
            oh = lane == ik
            onehots.append(oh)
            e_k.append(ik)
            w_k.append(jnp.sum(jnp.where(oh, scores, 0.0), axis=1, keepdims=True))
            masked = jnp.where(oh, neg, masked)
            sel = sel | oh
        wsum = w_k[0]
        for kk in range(1, TOP_K):
            wsum = wsum + w_k[kk]

        seld = sel.astype(F32)
        rk = jnp.dot(_tri(tm, strict=True).astype(BF16), seld.astype(BF16),
                     preferred_element_type=F32) + carry_ref[...]
        carry_ref[...] = carry_ref[...] + jnp.sum(seld, axis=0, keepdims=True)

        ol = lax.broadcasted_iota(jnp.int32, (tm, LANES), 1)
        packed = jnp.zeros((tm, LANES), F32)
        for kk in range(TOP_K):
            pos = jnp.sum(jnp.where(onehots[kk], rk, 0.0), axis=1, keepdims=True)
            wgt = w_k[kk] / wsum * ROUTED_SCALE
            packed = jnp.where(ol == _ROUTE_E + kk, e_k[kk], packed)
            packed = jnp.where(ol == _ROUTE_POS + kk, pos, packed)
            packed = jnp.where(ol == _ROUTE_W + kk, wgt, packed)
        route_ref[...] = packed


def _router(xpre, g, b, w_router, b_router, t_pad):
    t = xpre.shape[0]
    tm = _RT_TM
    n_tiles = t // tm
    last = n_tiles - 1
    assert t_pad % tm == 0 and t_pad > t
    return pl.pallas_call(
        functools.partial(_router_kernel, n_tiles=n_tiles),
        grid=(t_pad // tm,),
        in_specs=[pl.BlockSpec((tm, D_MODEL), lambda i: (jnp.minimum(i, last), 0)),
                  pl.BlockSpec((1, D_MODEL), lambda i: (0, 0)),
                  pl.BlockSpec((1, D_MODEL), lambda i: (0, 0)),
                  pl.BlockSpec((D_MODEL, N_EXPERTS), lambda i: (0, 0)),
                  pl.BlockSpec((1, N_EXPERTS), lambda i: (0, 0))],
        out_specs=[pl.BlockSpec((tm, D_MODEL), lambda i: (i, 0)),
                   pl.BlockSpec((tm, D_MODEL), lambda i: (i, 0)),
                   pl.BlockSpec((tm, LANES), lambda i: (i, 0)),
                   pl.BlockSpec((1, N_EXPERTS), lambda i: (0, 0))],
        out_shape=[jax.ShapeDtypeStruct((t_pad, D_MODEL), F32),
                   jax.ShapeDtypeStruct((t_pad, D_MODEL), BF16),
                   jax.ShapeDtypeStruct((t_pad, LANES), F32),
                   jax.ShapeDtypeStruct((1, N_EXPERTS), F32)],
        scratch_shapes=[pltpu.VMEM((1, N_EXPERTS), F32)],
        compiler_params=_cparams(("arbitrary",), VMEM_LIMIT),
        name="router",
    )(xpre, g, b, w_router, b_router)


def _row_copy(src_hbm, dst, sem, src_row, dst_row):
    return pltpu.make_async_copy(src_hbm.at[pl.ds(src_row, 1)], dst.at[pl.ds(dst_row, 1)], sem)


def _gather_kernel(nseg_ref, cnt_ref, idx_ref, idxn_ref, x_hbm, out_ref, buf, sem):
    s = pl.program_id(0)
    nseg = nseg_ref[0]

    def fetch(slot, idx, n):
        def zero(g, carry):
            buf[slot, pl.ds(pl.multiple_of(g * 8, 8), 8), :] = jnp.zeros((8, D_MODEL), F32)
            return carry

        lax.fori_loop(n // 8, SEG // 8, zero, 0)

        def issue(r, carry):
            _row_copy(x_hbm, buf.at[slot], sem.at[slot], idx[0, 0, r], r).start()
            return carry

        lax.fori_loop(0, n, issue, 0)

    @pl.when(s == 0)
    def _():
        fetch(0, idx_ref, cnt_ref[0])

    @pl.when(s + 1 < nseg)
    def _():
        fetch((s + 1) % 2, idxn_ref, cnt_ref[s + 1])

    @pl.when(s < nseg)
    def _():
        slot = s % 2

        def drain(r, carry):
            _row_copy(x_hbm, buf.at[slot], sem.at[slot], 0, r).wait()
            return carry

        lax.fori_loop(0, cnt_ref[s], drain, 0)
        out_ref[...] = buf[slot].astype(BF16)

    @pl.when(s >= nseg)
    def _():
        out_ref[...] = jnp.zeros_like(out_ref)


def _gather_rows(nseg, seg_cnt, slot_tok3, x1, nseg_max):
    last = nseg_max - 1
    return pl.pallas_call(
        _gather_kernel,
        grid_spec=pltpu.PrefetchScalarGridSpec(
            num_scalar_prefetch=2,
            grid=(nseg_max,),
            in_specs=[pl.BlockSpec((1, 1, SEG), lambda s, n, c: (s, 0, 0), memory_space=pltpu.SMEM),
                      pl.BlockSpec((1, 1, SEG), lambda s, n, c: (jnp.minimum(s + 1, last), 0, 0),
                                   memory_space=pltpu.SMEM),
                      pl.BlockSpec(memory_space=pl.ANY)],
            out_specs=pl.BlockSpec((SEG, D_MODEL), lambda s, n, c: (s, 0)),
            scratch_shapes=[pltpu.VMEM((2, SEG, D_MODEL), F32), pltpu.SemaphoreType.DMA((2,))],
        ),
        out_shape=jax.ShapeDtypeStruct((nseg_max * SEG, D_MODEL), BF16),
        compiler_params=_cparams(("arbitrary",), VMEM_LIMIT),
        name="gather_rows",
    )(nseg, seg_cnt, slot_tok3, slot_tok3, x1)


def _experts_kernel(nseg_ref, e_ref, y0_ref, nb_ref, z0_ref, nz_ref, x_ref, wg_hbm, wu_hbm, wd_hbm, y_hbm,
                    h_ref, wgb_ref, wub_ref, wdb_ref, abuf, bbuf, ybuf, sem_a, sem_b, sem_y, sem_z, *, nseg_max):
    s = pl.program_id(0)
    nseg = nseg_ref[0]
    valid = s < nseg
    e = e_ref[s]

    def a_copy(ex, c):
        w = wg_hbm if c % 2 == 0 else wu_hbm
        slot = c % A_SLOTS
        return pltpu.make_async_copy(w.at[ex, :, pl.ds((c // 2) * F_CHUNK, F_CHUNK)], abuf.at[slot],
                                     sem_a.at[slot])

    def b_copy(ex, n):
        slot = n % B_SLOTS
        return pltpu.make_async_copy(wd_hbm.at[ex, :, pl.ds(n * D_CHUNK, D_CHUNK)], bbuf.at[slot],
                                     sem_b.at[slot])

    def y_copy(ys, half, yblk, n, sem):
        return pltpu.make_async_copy(ybuf.at[ys, pl.ds(half * YBLK, YBLK), :],
                                     y_hbm.at[pl.ds(yblk * YBLK, YBLK), pl.ds(n * D_CHUNK, D_CHUNK)], sem)

    def wait_out(ys, nbv):
        y_copy(ys, 0, 0, 0, sem_y.at[ys]).wait()

        @pl.when(nbv == 2)
        def _():
            y_copy(ys, 1, 0, 0, sem_y.at[ys]).wait()

    @pl.when(s == 0)
    def _():
        for c in range(A_SLOTS):
            a_copy(e, c).start()
        for n in range(B_SLOTS):
            b_copy(e, n).start()

    @pl.when(valid)
    def _():
        has_next = s + 1 < nseg
        e_next = e_ref[jnp.minimum(s + 1, nseg_max - 1)]
        nbs = nb_ref[s]
        y0 = y0_ref[s]
        x = x_ref[...]

        for f in range(N_FCH):
            for c, dst in ((2 * f, wgb_ref), (2 * f + 1, wub_ref)):
                a_copy(e, c).wait()
                dst[...] = abuf[c % A_SLOTS].astype(BF16)
                nxt = c + A_SLOTS
                if nxt < N_ACH:
                    a_copy(e, nxt).start()
                else:
                    @pl.when(has_next)
                    def _():
                        a_copy(e_next, nxt - N_ACH).start()
            a = jnp.dot(x, wgb_ref[...], preferred_element_type=F32)
            u = jnp.dot(x, wub_ref[...], preferred_element_type=F32)
            h_ref[:, f * F_CHUNK:(f + 1) * F_CHUNK] = (_silu(a) * u).astype(BF16)

        h = h_ref[...]
        for n in range(N_DCH):
            b_copy(e, n).wait()
            wdb_ref[...] = bbuf[n % B_SLOTS].astype(BF16)
            nxt = n + B_SLOTS
            if nxt < N_DCH:
                b_copy(e, nxt).start()
            else:
                @pl.when(has_next)
                def _():
                    b_copy(e_next, nxt - N_DCH).start()
            ys = n % 2
            if n >= 2:
                wait_out(ys, nbs)
            else:
                @pl.when(s > 0)
                def _():
                    wait_out(ys, nb_ref[jnp.maximum(s - 1, 0)])
            ybuf[ys] = jnp.dot(h, wdb_ref[...], preferred_element_type=F32)
            y_copy(ys, 0, y0, n, sem_y.at[ys]).start()

            @pl.when(nbs == 2)
            def _():
                y_copy(ys, 1, y0 + 1, n, sem_y.at[ys]).start()

        @pl.when(s == nseg_max - 1)
        def _():
            wait_out(0, nbs)
            wait_out(1, nbs)

    @pl.when(s == nseg)
    def _():
        nbp = nb_ref[jnp.maximum(s - 1, 0)]
        wait_out(0, nbp)
        wait_out(1, nbp)

    @pl.when(jnp.logical_not(valid))
    def _():
        nz = nz_ref[s]
        z0 = z0_ref[s]

        @pl.when(nz > 0)
        def _():
            ybuf[0] = jnp.zeros((SEG, D_CHUNK), F32)
            for wait in (False, True):
                for q in range(ZFILL_PER_STEP):
                    @pl.when(q < nz)
                    def _():
                        for n in range(N_DCH):
                            cp = y_copy(0, 0, z0 + q, n, sem_z)
                            cp.wait() if wait else cp.start()


def _experts(sched, x_sorted, w_gate, w_up, w_down, nseg_max, n_yrows):
    def im_x(s, nseg, *_):
        return (jnp.minimum(s, nseg[0] - 1), 0)

    any_spec = pl.BlockSpec(memory_space=pl.ANY)
    return pl.pallas_call(
        functools.partial(_experts_kernel, nseg_max=nseg_max),
        grid_spec=pltpu.PrefetchScalarGridSpec(
            num_scalar_prefetch=6,
            grid=(nseg_max,),
            in_specs=[pl.BlockSpec((SEG, D_MODEL), im_x), any_spec, any_spec, any_spec],
            out_specs=any_spec,
            scratch_shapes=[pltpu.VMEM((SEG, EXPERT_DIM), BF16),
                            pltpu.VMEM((D_MODEL, F_CHUNK), BF16),
                            pltpu.VMEM((D_MODEL, F_CHUNK), BF16),
                            pltpu.VMEM((EXPERT_DIM, D_CHUNK), BF16),
                            pltpu.VMEM((A_SLOTS, D_MODEL, F_CHUNK), F32),
                            pltpu.VMEM((B_SLOTS, EXPERT_DIM, D_CHUNK), F32),
                            pltpu.VMEM((2, SEG, D_CHUNK), F32),
                            pltpu.SemaphoreType.DMA((A_SLOTS,)),
                            pltpu.SemaphoreType.DMA((B_SLOTS,)),
                            pltpu.SemaphoreType.DMA((2,)),
                            pltpu.SemaphoreType.DMA(())],
        ),
        out_shape=jax.ShapeDtypeStruct((n_yrows, D_MODEL), F32),
        compiler_params=_cparams(("arbitrary",), VMEM_LIMIT),
        name="experts",
    )(sched["nseg"], sched["e"], sched["y0"], sched["nb"], sched["z0"], sched["nz"],
      x_sorted, w_gate, w_up, w_down)


def _pick(onehot, arr):
    return jnp.sum(jnp.where(onehot, arr, 0), axis=-1)


def _routed_schedule(cnt, nseg_max, nyb_max):
    nsg = (cnt + SEG - 1) // SEG
    nyb = (cnt + YBLK - 1) // YBLK
    seg_end = jnp.cumsum(nsg)
    seg_start = seg_end - nsg
    yb_start = jnp.cumsum(nyb) - nyb
    nseg = seg_end[-1]
    nyb_tot = jnp.sum(nyb)
    sidx = jnp.arange(nseg_max, dtype=jnp.int32)
    sc = jnp.minimum(sidx, nseg - 1)
    seg_e = jnp.minimum(jnp.sum((seg_end[None, :] <= sc[:, None]).astype(jnp.int32), axis=1), N_EXPERTS - 1)
    oh = seg_e[:, None] == jnp.arange(N_EXPERTS, dtype=jnp.int32)[None, :]
    loc = sc - _pick(oh, seg_start)
    live = sidx < nseg
    spare = sidx - nseg
    z0 = nyb_tot + ZFILL_PER_STEP * spare
    sched = {
        "nseg": nseg.reshape(1).astype(jnp.int32),
        "e": seg_e.astype(jnp.int32),
        "y0": (_pick(oh, yb_start) + (SEG // YBLK) * loc).astype(jnp.int32),
        "nb": jnp.where(live, jnp.clip(_pick(oh, nyb) - (SEG // YBLK) * loc, 0, SEG // YBLK), 0).astype(jnp.int32),
        "z0": jnp.clip(z0, 0, nyb_max - 1).astype(jnp.int32),
        "nz": jnp.where(live, 0, jnp.clip(nyb_max - z0, 0, ZFILL_PER_STEP)).astype(jnp.int32),
    }
    seg_cnt = jnp.where(live, jnp.clip(_pick(oh, cnt) - SEG * loc, 0, SEG), 0).astype(jnp.int32)
    return sched, seg_cnt, seg_start, yb_start


def _dense_schedule(nseg):
    sidx = jnp.arange(nseg, dtype=jnp.int32)
    zero = jnp.zeros((nseg,), jnp.int32)
    return {"nseg": jnp.full((1,), nseg, jnp.int32), "e": zero, "y0": (SEG // YBLK) * sidx,
            "nb": jnp.full((nseg,), SEG // YBLK, jnp.int32), "z0": zero, "nz": zero}


def _combine_kernel(slot_ref, slotn_ref, w_ref, ysh_ref, x1_ref, g_ref, b_ref, y_hbm, out_ref, buf, sem,
                    *, n_steps):
    i = pl.program_id(0)
    tm = COMB_TM
    n = tm * TOP_K

    def issue(slot, idx):
        def body(j, carry):
            _row_copy(y_hbm, buf.at[slot], sem.at[slot], idx[0, 0, j], j).start()
            return carry

        lax.fori_loop(0, n, body, 0, unroll=8)

    @pl.when(i == 0)
    def _():
        issue(0, slot_ref)

    @pl.when(i + 1 < n_steps)
    def _():
        issue((i + 1) % 2, slotn_ref)

    cur = i % 2

    def drain(j, carry):
        _row_copy(y_hbm, buf.at[cur], sem.at[cur], 0, j).wait()
        return carry

    lax.fori_loop(0, n, drain, 0, unroll=8)
    w = w_ref[...]
    acc = w[:, 0:1] * buf[cur, 0:tm, :]
    for kk in range(1, TOP_K):
        acc = acc + w[:, kk:kk + 1] * buf[cur, kk * tm:(kk + 1) * tm, :]
    f = acc + ysh_ref[...]
    out_ref[...] = _ln_rows(ALPHA * x1_ref[...] + f, g_ref[...], b_ref[...])


def _combine(slots3, w, ysh, x1, g, b, y_slots, t):
    tm = COMB_TM
    n_steps = t // tm
    last = n_steps - 1
    return pl.pallas_call(
        functools.partial(_combine_kernel, n_steps=n_steps),
        grid=(n_steps,),
        in_specs=[pl.BlockSpec((1, 1, tm * TOP_K), lambda i: (i, 0, 0), memory_space=pltpu.SMEM),
                  pl.BlockSpec((1, 1, tm * TOP_K), lambda i: (jnp.minimum(i + 1, last), 0, 0),
                               memory_space=pltpu.SMEM),
                  pl.BlockSpec((tm, TOP_K), lambda i: (i, 0)),
                  pl.BlockSpec((tm, D_MODEL), lambda i: (i, 0)),
                  pl.BlockSpec((tm, D_MODEL), lambda i: (i, 0)),
                  pl.BlockSpec((1, D_MODEL), lambda i: (0, 0)),
                  pl.BlockSpec((1, D_MODEL), lambda i: (0, 0)),
                  pl.BlockSpec(memory_space=pl.ANY)],
        out_specs=pl.BlockSpec((tm, D_MODEL), lambda i: (i, 0)),
        out_shape=jax.ShapeDtypeStruct((t, D_MODEL), F32),
        scratch_shapes=[pltpu.VMEM((2, tm * TOP_K, D_MODEL), F32), pltpu.SemaphoreType.DMA((2,))],
        compiler_params=_cparams(("arbitrary",), VMEM_LIMIT),
        name="combine",
    )(slots3, slots3, w, ysh, x1, g, b, y_slots)


def _moe(x1, x1b, route, counts, t_tok, t_pad, w_gate, w_up, w_down, w_sh_gate, w_sh_up, w_sh_down, g2, b2):
    n_assign = t_tok * TOP_K
    nseg_max = (n_assign + N_EXPERTS * (SEG - 1)) // SEG
    nyb_max = (n_assign + N_EXPERTS * (YBLK - 1)) // YBLK
    eidx = route[:t_tok, _ROUTE_E:_ROUTE_E + TOP_K].astype(jnp.int32)
    pos = route[:t_tok, _ROUTE_POS:_ROUTE_POS + TOP_K].astype(jnp.int32)
    wts = route[:t_tok, _ROUTE_W:_ROUTE_W + TOP_K]
    sched, seg_cnt, seg_start, yb_start = _routed_schedule(counts[0].astype(jnp.int32), nseg_max, nyb_max)
    oh_e = eidx[..., None] == jnp.arange(N_EXPERTS, dtype=jnp.int32)
    xslot = _pick(oh_e, seg_start) * SEG + pos
    yslot = _pick(oh_e, yb_start) * YBLK + pos
    tok = jnp.broadcast_to(jnp.arange(t_tok, dtype=jnp.int32)[:, None], (t_tok, TOP_K))
    slot_tok = jnp.zeros((nseg_max * SEG,), jnp.int32).at[xslot.reshape(-1)].set(tok.reshape(-1))
    x_sorted = _gather_rows(sched["nseg"], seg_cnt, slot_tok.reshape(nseg_max, 1, SEG), x1, nseg_max)

    y_slots = _experts(sched, x_sorted, w_gate, w_up, w_down, nseg_max, nyb_max * YBLK)
    ysh = _experts(_dense_schedule(t_pad // SEG), x1b, w_sh_gate, w_sh_up, w_sh_down, t_pad // SEG, t_pad)

    n_ct = t_tok // COMB_TM
    slots3 = yslot.reshape(n_ct, COMB_TM, TOP_K).transpose(0, 2, 1).reshape(n_ct, 1, COMB_TM * TOP_K)
    return _combine(slots3, wts, ysh, x1, g2, b2, y_slots, t_tok)


def kernel(x_prompt, x_sample, state_gla, state_conv, meta_tokens, ln_emb_g, ln_emb_b, w_in, w_alpha_up, b_alpha, conv_w, gla_norm_g, w_o, ln1_g, ln1_b, w_router, b_router, w_exp_gate, w_exp_up, w_exp_down, w_sh_gate, w_sh_up, w_sh_down, ln2_g, ln2_b):
    bp, seq, d = x_prompt.shape
    bs = x_sample.shape[0]
    assert d == D_MODEL and x_sample.shape[1] == 1 and w_in.shape[0] == 1
    tp = bp * seq
    row_s = tp
    row_m = tp + bs
    t_tok = tp + bs
    t_all = -(-(row_m + N_META) // 256) * 256
    assert seq % 1024 == 0 and bs == 128 and t_tok % 640 == 0 and t_all % 768 == 0

    row = lambda a: a.reshape(1, -1)

    x_all = jnp.concatenate([x_prompt.reshape(tp, d), x_sample.reshape(bs, d), meta_tokens.astype(F32),
                             jnp.zeros((t_all - row_m - N_META, d), F32)], axis=0)
    lr0 = COL_V + GLA_HEADS * GLA_DV
    w_in0 = w_in[0]
    w_main = jnp.concatenate([w_in0[:, :lr0], w_in0[:, lr0 + GATE_RANK:]], axis=1).astype(BF16)
    w_lr = jnp.pad(w_in0[:, lr0:lr0 + GATE_RANK], ((0, 0), (0, LANES - GATE_RANK))).astype(BF16)
    w_au = jnp.pad(w_alpha_up[0], ((0, LANES - GATE_RANK), (0, 0)))
    xn, xnb, la = _ln_in(x_all, row(ln_emb_g), row(ln_emb_b), w_lr, w_au, row(b_alpha[0]))
    proj = _in_proj(xnb, w_main)

    gain = row(gla_norm_g[0])
    o_p, gla_p = _gla_prompt(proj, la, gain, bp, seq, row_m)

    def cols(a):
        return a.reshape(bs // _SAMP_GRP, _SAMP_GRP, GLA_HEADS, GLA_DK).transpose(0, 2, 3, 1)

    ps = proj[row_s:row_s + bs]
    gla_s, o_s = _gla_sample(state_gla[0], cols(ps[:, COL_Q:COL_Q + QK_WIDTH]),
                             cols(ps[:, COL_K:COL_K + QK_WIDTH]), cols(la[row_s:row_s + bs]),
                             proj, gain, bs, row_s)

    cw = conv_w[0]
    m_p, cs_p = _merge_prompt(o_p, proj, cw, bp, seq, row_m)
    m_s, u_s = _merge_sample(o_s, proj, cw, state_conv[0].reshape(bs, (CONV_K - 1) * d), bs, row_s)
    conv_p = cs_p.reshape(bp, 8, d)[:, 8 - (CONV_K - 1):, :]
    conv_s = jnp.stack([state_conv[0][:, 1, :], u_s], axis=1)

    merged = jnp.concatenate([m_p, m_s], axis=0)
    xpre = _out_proj(merged, w_o[0].astype(BF16), xn)
    t_pad = -(-t_tok // SEG) * SEG
    x1, x1b, route, counts = _router(xpre, row(ln1_g[0]), row(ln1_b[0]), w_router[0], row(b_router[0]), t_pad)

    y = _moe(x1, x1b, route, counts, t_tok, t_pad, w_exp_gate[0], w_exp_up[0], w_exp_down[0],
             w_sh_gate, w_sh_up, w_sh_down, row(ln2_g[0]), row(ln2_b[0]))

    y_prompt = y[:tp].reshape(bp, seq, d)
    y_sample = y[tp:].reshape(bs, 1, d)
    return (y_prompt, y_sample, gla_p[None], conv_p[None], gla_s[None], conv_s[None])
```

```python
import functools

import jax
import jax.numpy as jnp
from jax import lax
from jax.experimental import pallas as pl
from jax.experimental.pallas import tpu as pltpu

F32 = jnp.float32
BF16 = jnp.bfloat16
HIGHEST = lax.Precision.HIGHEST

D_MODEL = 4096
N_META = 16
GLA_HEADS = 8
GLA_DK = 256
GLA_DV = 512
QK_WIDTH = GLA_HEADS * GLA_DK
GATE_RANK = 16
GATE_TAU = 16.0
GLA_CHUNK = 64
GLA_SUB = 16
CONV_K = 3
N_EXPERTS = 256
TOP_K = 8
N_GROUPS = 8
GROUP_SIZE = N_EXPERTS // N_GROUPS
TOPK_GROUPS = 4
EXPERT_DIM = 1024
ROUTED_SCALE = 2.5
ALPHA = 2.0 ** 0.25
EPS = 1e-5

COL_Q, COL_K, COL_V, COL_G, COL_CB, COL_CC, COL_CH, COL_GA, COL_GC = (
    0, 2048, 4096, 8192, 12288, 16384, 20480, 24576, 28672)
N_MAIN = 32768

LANES = 128
SEG = 512
YBLK = 256
ROW_STEP = 128
F_CHUNK = 256
N_FCH = EXPERT_DIM // F_CHUNK
N_ACH = 2 * N_FCH
D_CHUNK = 1024
N_DCH = D_MODEL // D_CHUNK
A_SLOTS = 4
B_SLOTS = 2
ZFILL_PER_STEP = 4
COMB_TM = 64
VMEM_LIMIT = 60 * 1024 * 1024


def _cparams(sem, vmem=None):
    return pltpu.CompilerParams(dimension_semantics=sem, vmem_limit_bytes=vmem)


def _ln_rows(x, g, b):
    mu = jnp.mean(x, axis=-1, keepdims=True)
    xc = x - mu
    var = jnp.mean(xc * xc, axis=-1, keepdims=True)
    return xc * lax.rsqrt(var + EPS) * g + b


def _sigmoid(x):
    return 1.0 / (1.0 + jnp.exp(-x))


def _silu(x):
    return x * _sigmoid(x)


def _ln_in_kernel(x_ref, g_ref, b_ref, wlr_ref, wau_ref, ba_ref, xn_ref, xnb_ref, la_ref):
    xn = _ln_rows(x_ref[...], g_ref[...], b_ref[...])
    xn_ref[...] = xn
    xb = xn.astype(BF16)
    xnb_ref[...] = xb
    lr = jnp.dot(xb, wlr_ref[...], preferred_element_type=F32)
    z = jnp.dot(lr, wau_ref[...], precision=HIGHEST, preferred_element_type=F32) + ba_ref[...]
    la_ref[...] = (jnp.minimum(z, 0.0) - jnp.log1p(jnp.exp(-jnp.abs(z)))) * (1.0 / GATE_TAU)


def _ln_in(x_all, g, b, wlr, wau, ba, tm=256):
    t = x_all.shape[0]
    return pl.pallas_call(
        _ln_in_kernel,
        grid=(t // tm,),
        in_specs=[
            pl.BlockSpec((tm, D_MODEL), lambda i: (i, 0)),
            pl.BlockSpec((1, D_MODEL), lambda i: (0, 0)),
            pl.BlockSpec((1, D_MODEL), lambda i: (0, 0)),
            pl.BlockSpec((D_MODEL, LANES), lambda i: (0, 0)),
            pl.BlockSpec((LANES, QK_WIDTH), lambda i: (0, 0)),
            pl.BlockSpec((1, QK_WIDTH), lambda i: (0, 0)),
        ],
        out_specs=[
            pl.BlockSpec((tm, D_MODEL), lambda i: (i, 0)),
            pl.BlockSpec((tm, D_MODEL), lambda i: (i, 0)),
            pl.BlockSpec((tm, QK_WIDTH), lambda i: (i, 0)),
        ],
        out_shape=[
            jax.ShapeDtypeStruct((t, D_MODEL), F32),
            jax.ShapeDtypeStruct((t, D_MODEL), BF16),
            jax.ShapeDtypeStruct((t, QK_WIDTH), F32),
        ],
        compiler_params=_cparams(("arbitrary",), VMEM_LIMIT),
        name="ln_in",
    )(x_all, g, b, wlr, wau, ba)


_RP_TN = 512


def _repack_kernel(wa_ref, wb_ref, o_ref, *, first_shifted):
    j = pl.program_id(0)
    tn = _RP_TN

    @pl.when(j < first_shifted)
    def _():
        o_ref[...] = wa_ref[0].astype(BF16)

    @pl.when(j >= first_shifted)
    def _():
        ra = pltpu.roll(wa_ref[0], tn - GATE_RANK, 1)
        rb = pltpu.roll(wb_ref[0], LANES - GATE_RANK, 1)
        lane = lax.broadcasted_iota(jnp.int32, (D_MODEL, LANES), 1)
        o_ref[:, 0:tn - LANES] = ra[:, 0:tn - LANES].astype(BF16)
        o_ref[:, tn - LANES:tn] = jnp.where(lane < LANES - GATE_RANK, ra[:, tn - LANES:tn], rb).astype(BF16)


def _repack_w_in(w_in, lr0):
    tn = _RP_TN
    assert lr0 % tn == 0 and N_MAIN % tn == 0 and w_in.shape[2] == N_MAIN + GATE_RANK
    per = tn // LANES
    return pl.pallas_call(
        functools.partial(_repack_kernel, first_shifted=lr0 // tn),
        grid=(N_MAIN // tn,),
        in_specs=[pl.BlockSpec((1, D_MODEL, tn), lambda j: (0, 0, j)),
                  pl.BlockSpec((1, D_MODEL, LANES), lambda j: (0, 0, per * (j + 1)))],
        out_specs=pl.BlockSpec((D_MODEL, tn), lambda j: (0, j)),
        out_shape=jax.ShapeDtypeStruct((D_MODEL, N_MAIN), BF16),
        compiler_params=_cparams(("arbitrary",), VMEM_LIMIT),
        name="repack_w_in",
    )(w_in, w_in)


def _mm_kernel(x_ref, w_ref, o_ref):
    o_ref[...] = jnp.dot(x_ref[...], w_ref[...], preferred_element_type=F32)


def _in_proj(xb, w, tm=768, tn=512):
    t, k = xb.shape
    n = w.shape[1]
    return pl.pallas_call(
        _mm_kernel,
        grid=(t // tm, n // tn),
        in_specs=[pl.BlockSpec((tm, k), lambda i, j: (i, 0)),
                  pl.BlockSpec((k, tn), lambda i, j: (0, j))],
        out_specs=pl.BlockSpec((tm, tn), lambda i, j: (i, j)),
        out_shape=jax.ShapeDtypeStruct((t, n), F32),
        compiler_params=_cparams(("arbitrary", "arbitrary"), VMEM_LIMIT),
        name="in_proj",
    )(xb, w)


def _tri(n, strict=False):
    r = lax.broadcasted_iota(jnp.int32, (n, n), 0)
    c = lax.broadcasted_iota(jnp.int32, (n, n), 1)
    return ((r > c) if strict else (r >= c)).astype(F32)


_NT = (((1,), (1,)), ((), ()))
_TN = (((0,), (0,)), ((), ()))


def _gla_prompt_kernel(q_ref, k_ref, v_ref, la_ref, km_ref, vm_ref, lam_ref, gain_ref,
                       o_ref, sout_ref, st_ref, *, n_chunks):
    c = pl.program_id(2)

    @pl.when(c == 0)
    def _():
        bm = jnp.dot(_tri(N_META), lam_ref[...], precision=HIGHEST, preferred_element_type=F32)
        ktm = km_ref[...] * jnp.exp(bm[N_META - 1:N_META, :] - bm)
        st_ref[...] = lax.dot_general(vm_ref[...].astype(BF16), ktm.astype(BF16), _TN,
                                      preferred_element_type=F32)

    cs = GLA_CHUNK
    b = jnp.dot(_tri(cs), la_ref[...], precision=HIGHEST, preferred_element_type=F32)
    q = q_ref[...] * (GLA_DK ** -0.5)
    k = k_ref[...]
    vb = v_ref[...].astype(BF16)
    st = st_ref[...]
    o = lax.dot_general((q * jnp.exp(b)).astype(BF16), st.astype(BF16), _NT, preferred_element_type=F32)

    lane = lax.broadcasted_iota(jnp.int32, (GLA_SUB, cs), 1)
    row = lax.broadcasted_iota(jnp.int32, (GLA_SUB, cs), 0)
    blocks = []
    for blk in range(cs // GLA_SUB):
        r0 = blk * GLA_SUB
        qi = q[r0:r0 + GLA_SUB]
        bi = b[r0:r0 + GLA_SUB]
        if blk > 0:
            bref = b[r0 - 1:r0, :]
            qt = qi * jnp.exp(bi - bref)
            kt = k * jnp.exp(jnp.minimum(bref - b, 0.0))
            s = lax.dot_general(qt.astype(BF16), kt.astype(BF16), _NT, preferred_element_type=F32)
            s = jnp.where(lane < r0, s, 0.0)
        else:
            s = jnp.zeros((GLA_SUB, cs), F32)
        for j in range(GLA_SUB):
            kj = k[r0 + j:r0 + j + 1, :]
            bj = b[r0 + j:r0 + j + 1, :]
            col = jnp.sum(qi * kj * jnp.exp(jnp.minimum(bi - bj, 0.0)), axis=1, keepdims=True)
            s = jnp.where((lane == r0 + j) & (row >= j), col, s)
        blocks.append(s)
    scores = jnp.concatenate(blocks, axis=0)
    o = o + jnp.dot(scores.astype(BF16), vb, preferred_element_type=F32)
    o = o * lax.rsqrt(jnp.mean(o * o, axis=-1, keepdims=True) + EPS) * gain_ref[...]
    o_ref[...] = o

    bl = b[cs - 1:cs, :]
    kt2 = k * jnp.exp(bl - b)
    st_new = st * jnp.exp(bl) + lax.dot_general(vb, kt2.astype(BF16), _TN, preferred_element_type=F32)
    st_ref[...] = st_new

    @pl.when(c == n_chunks - 1)
    def _():
        sout_ref[0, 0] = st_new.T


def _gla_prompt(proj, la, gain, bp, seq, row_meta):
    n_chunks = seq // GLA_CHUNK
    cq, ck, cv = COL_Q // GLA_DK, COL_K // GLA_DK, COL_V // GLA_DV
    mrow = row_meta // N_META
    return pl.pallas_call(
        functools.partial(_gla_prompt_kernel, n_chunks=n_chunks),
        grid=(bp, GLA_HEADS, n_chunks),
        in_specs=[
            pl.BlockSpec((GLA_CHUNK, GLA_DK), lambda b, h, c: (b * n_chunks + c, cq + h)),
            pl.BlockSpec((GLA_CHUNK, GLA_DK), lambda b, h, c: (b * n_chunks + c, ck + h)),
            pl.BlockSpec((GLA_CHUNK, GLA_DV), lambda b, h, c: (b * n_chunks + c, cv + h)),
            pl.BlockSpec((GLA_CHUNK, GLA_DK), lambda b, h, c: (b * n_chunks + c, h)),
            pl.BlockSpec((N_META, GLA_DK), lambda b, h, c: (mrow, ck + h)),
            pl.BlockSpec((N_META, GLA_DV), lambda b, h, c: (mrow, cv + h)),
            pl.BlockSpec((N_META, GLA_DK), lambda b, h, c: (mrow, h)),
            pl.BlockSpec((1, GLA_DV), lambda b, h, c: (0, h)),
        ],
        out_specs=[
            pl.BlockSpec((GLA_CHUNK, GLA_DV), lambda b, h, c: (b * n_chunks + c, h)),
            pl.BlockSpec((1, 1, GLA_DK, GLA_DV), lambda b, h, c: (b, h, 0, 0)),
        ],
        out_shape=[
            jax.ShapeDtypeStruct((bp * seq, D_MODEL), F32),
            jax.ShapeDtypeStruct((bp, GLA_HEADS, GLA_DK, GLA_DV), F32),
        ],
        scratch_shapes=[pltpu.VMEM((GLA_DV, GLA_DK), F32)],
        compiler_params=_cparams(("arbitrary", "arbitrary", "arbitrary")),
        name="gla_prompt",
    )(proj, proj, proj, la, proj, proj, la, gain)


_SAMP_GRP = 8


def _gla_sample_kernel(s_ref, qc_ref, kc_ref, lac_ref, v_ref, gain_ref, so_ref, o_ref):
    qc = qc_ref[0, 0] * (GLA_DK ** -0.5)
    kc = kc_ref[0, 0]
    ac = jnp.exp(lac_ref[0, 0])
    v = v_ref[...]
    rows = []
    for i in range(_SAMP_GRP):
        s_new = ac[:, i:i + 1] * s_ref[i, 0] + kc[:, i:i + 1] * v[i:i + 1, :]
        so_ref[i, 0] = s_new
        rows.append(jnp.sum(qc[:, i:i + 1] * s_new, axis=0, keepdims=True))
    o = jnp.concatenate(rows, axis=0)
    o_ref[...] = o * lax.rsqrt(jnp.mean(o * o, axis=-1, keepdims=True) + EPS) * gain_ref[...]


def _gla_sample(state, qc, kc, lac, proj, gain, bs, row_s):
    ng = bs // _SAMP_GRP
    cv = COL_V // GLA_DV
    r0 = row_s // _SAMP_GRP
    col_spec = pl.BlockSpec((1, 1, GLA_DK, _SAMP_GRP), lambda g, h: (g, h, 0, 0))
    st_spec = pl.BlockSpec((_SAMP_GRP, 1, GLA_DK, GLA_DV), lambda g, h: (g, h, 0, 0))
    return pl.pallas_call(
        _gla_sample_kernel,
        grid=(ng, GLA_HEADS),
        in_specs=[st_spec, col_spec, col_spec, col_spec,
                  pl.BlockSpec((_SAMP_GRP, GLA_DV), lambda g, h: (r0 + g, cv + h)),
                  pl.BlockSpec((1, GLA_DV), lambda g, h: (0, h))],
        out_specs=[st_spec, pl.BlockSpec((_SAMP_GRP, GLA_DV), lambda g, h: (g, h))],
        out_shape=[jax.ShapeDtypeStruct(state.shape, F32),
                   jax.ShapeDtypeStruct((bs, D_MODEL), F32)],
        compiler_params=_cparams(("arbitrary", "arbitrary"), VMEM_LIMIT),
        name="gla_sample",
    )(state, qc, kc, lac, proj, gain)


def _merge(o, g, cb, yconv, ga, gc):
    return _sigmoid(ga) * (o * _silu(g)) + _sigmoid(gc) * (cb * yconv)


def _merge_prompt_kernel(o_ref, g_ref, cb_ref, cc_ref, ch_ref, ga_ref, gc_ref, cw_ref, ccm_ref, chm_ref,
                         m_ref, cs_ref, carry_ref, *, n_rt):
    r = pl.program_id(2)
    rows = cc_ref.shape[0]

    @pl.when(r == 0)
    def _():
        um = ccm_ref[...] * chm_ref[...]
        carry_ref[...] = um[N_META - 8:N_META, :]

    u = cc_ref[...] * ch_ref[...]
    carry = carry_ref[...]
    p1 = carry[7:8, :]
    p2 = carry[6:7, :]
    ri = lax.broadcasted_iota(jnp.int32, u.shape, 0)
    u1 = jnp.where(ri == 0, p1, pltpu.roll(u, 1, 0))
    u2 = jnp.where(ri == 0, p2, jnp.where(ri == 1, p1, pltpu.roll(u, 2, 0)))
    cw = cw_ref[...]
    yconv = cw[0:1, :] * u2 + cw[1:2, :] * u1 + cw[2:3, :] * u
    m_ref[...] = _merge(o_ref[...], g_ref[...], cb_ref[...], yconv, ga_ref[...], gc_ref[...]).astype(BF16)
    tail = u[rows - 8:rows, :]
    carry_ref[...] = tail

    @pl.when(r == n_rt - 1)
    def _():
        cs_ref[...] = tail


def _merge_prompt(o, proj, conv_w, bp, seq, row_meta, rows=1024, cols=512):
    n_rt = seq // rows
    ncb = D_MODEL // cols
    mrow = row_meta // N_META

    def pspec(col0):
        return pl.BlockSpec((rows, cols), lambda b, j, r: (b * n_rt + r, col0 // cols + j))

    def mspec(col0):
        return pl.BlockSpec((N_META, cols), lambda b, j, r: (mrow, col0 // cols + j))

    return pl.pallas_call(
        functools.partial(_merge_prompt_kernel, n_rt=n_rt),
        grid=(bp, ncb, n_rt),
        in_specs=[pspec(0), pspec(COL_G), pspec(COL_CB), pspec(COL_CC), pspec(COL_CH), pspec(COL_GA),
                  pspec(COL_GC), pl.BlockSpec((CONV_K, cols), lambda b, j, r: (0, j)),
                  mspec(COL_CC), mspec(COL_CH)],
        out_specs=[pl.BlockSpec((rows, cols), lambda b, j, r: (b * n_rt + r, j)),
                   pl.BlockSpec((8, cols), lambda b, j, r: (b, j))],
        out_shape=[jax.ShapeDtypeStruct((bp * seq, D_MODEL), BF16),
                   jax.ShapeDtypeStruct((bp * 8, D_MODEL), F32)],
        scratch_shapes=[pltpu.VMEM((8, cols), F32)],
        compiler_params=_cparams(("arbitrary", "arbitrary", "arbitrary"), VMEM_LIMIT),
        name="merge_prompt",
    )(o, proj, proj, proj, proj, proj, proj, conv_w, proj, proj)


def _merge_sample_kernel(o_ref, g_ref, cb_ref, cc_ref, ch_ref, ga_ref, gc_ref, cw_ref, p0_ref, p1_ref,
                         m_ref, u_ref):
    u = cc_ref[...] * ch_ref[...]
    cw = cw_ref[...]
    yconv = cw[0:1, :] * p0_ref[...] + cw[1:2, :] * p1_ref[...] + cw[2:3, :] * u
    m_ref[...] = _merge(o_ref[...], g_ref[...], cb_ref[...], yconv, ga_ref[...], gc_ref[...]).astype(BF16)
    u_ref[...] = u


def _merge_sample(o, proj, conv_w, prefix2d, bs, row_s, cols=512):
    ncb = D_MODEL // cols
    rb = row_s // bs

    def pspec(col0):
        return pl.BlockSpec((bs, cols), lambda j: (rb, col0 // cols + j))

    return pl.pallas_call(
        _merge_sample_kernel,
        grid=(ncb,),
        in_specs=[pl.BlockSpec((bs, cols), lambda j: (0, j)), pspec(COL_G), pspec(COL_CB), pspec(COL_CC),
                  pspec(COL_CH), pspec(COL_GA), pspec(COL_GC),
                  pl.BlockSpec((CONV_K, cols), lambda j: (0, j)),
                  pl.BlockSpec((bs, cols), lambda j: (0, j)),
                  pl.BlockSpec((bs, cols), lambda j: (0, ncb + j))],
        out_specs=[pl.BlockSpec((bs, cols), lambda j: (0, j)),
                   pl.BlockSpec((bs, cols), lambda j: (0, j))],
        out_shape=[jax.ShapeDtypeStruct((bs, D_MODEL), BF16),
                   jax.ShapeDtypeStruct((bs, D_MODEL), F32)],
        compiler_params=_cparams(("arbitrary",)),
        name="merge_sample",
    )(o, proj, proj, proj, proj, proj, proj, conv_w, prefix2d, prefix2d)


def _wo_kernel(m_ref, w_ref, xn_ref, o_ref):
    o_ref[...] = ALPHA * xn_ref[...] + jnp.dot(m_ref[...], w_ref[...], preferred_element_type=F32)


def _out_proj(merged, w_o, xn, tm=640, tn=512):
    t = merged.shape[0]
    return pl.pallas_call(
        _wo_kernel,
        grid=(t // tm, D_MODEL // tn),
        in_specs=[pl.BlockSpec((tm, D_MODEL), lambda i, j: (i, 0)),
                  pl.BlockSpec((D_MODEL, tn), lambda i, j: (0, j)),
                  pl.BlockSpec((tm, tn), lambda i, j: (i, j))],
        out_specs=pl.BlockSpec((tm, tn), lambda i, j: (i, j)),
        out_shape=jax.ShapeDtypeStruct((t, D_MODEL), F32),
        compiler_params=_cparams(("arbitrary", "arbitrary"), VMEM_LIMIT),
        name="out_proj",
    )(merged, w_o, xn)


_RT_TM = 128
_ROUTE_E, _ROUTE_POS, _ROUTE_W = 0, TOP_K, 2 * TOP_K


def _router_kernel(xp_ref, g_ref, b_ref, wr_ref, br_ref, x1_ref, x1b_ref, route_ref, cnt_ref, carry_ref,
                   *, n_tiles):
    i = pl.program_id(0)

    @pl.when(i == 0)
    def _():
        carry_ref[...] = jnp.zeros_like(carry_ref)

    @pl.when(i >= n_tiles)
    def _():
        x1_ref[...] = jnp.zeros_like(x1_ref)
        x1b_ref[...] = jnp.zeros_like(x1b_ref)
        route_ref[...] = jnp.zeros_like(route_ref)

    @pl.when(i == n_tiles)
    def _():
        cnt_ref[...] = carry_ref[...]

    @pl.when(i < n_tiles)
    def _():
        tm = _RT_TM
        x1 = _ln_rows(xp_ref[...], g_ref[...], b_ref[...])
        x1_ref[...] = x1
        x1b_ref[...] = x1.astype(BF16)
        logits = jnp.dot(x1, wr_ref[...], precision=HIGHEST, preferred_element_type=F32)
        scores = _sigmoid(logits)
        choice = scores + br_ref[...]
        neg = -jnp.inf
        big = 1e9
        lane_i = lax.broadcasted_iota(jnp.int32, (tm, N_EXPERTS), 1)
        lane = lane_i.astype(F32)
        gid = lane_i // GROUP_SIZE

        gs = []
        for g in range(N_GROUPS):
            cg = jnp.where(gid == g, choice, neg)
            m1 = jnp.max(cg, axis=1, keepdims=True)
            i1 = jnp.min(jnp.where(cg == m1, lane, big), axis=1, keepdims=True)
            m2 = jnp.max(jnp.where(lane == i1, neg, cg), axis=1, keepdims=True)
            gs.append(m1 + m2)
        gsel = jnp.zeros((tm, N_EXPERTS), jnp.bool_)
        for g in range(N_GROUPS):
            rank = jnp.zeros((tm, 1), F32)
            for g2 in range(N_GROUPS):
                if g2 == g:
                    continue
                better = (gs[g2] >= gs[g]) if g2 < g else (gs[g2] > gs[g])
                rank = rank + better.astype(F32)
            gsel = gsel | ((gid == g) & (rank < TOPK_GROUPS))
        masked = jnp.where(gsel, choice, neg)

        sel = jnp.zeros((tm, N_EXPERTS), jnp.bool_)
        onehots, e_k, w_k = [], [], []
        for _ in range(TOP_K):
            m = jnp.max(masked, axis=1, keepdims=True)
            ik = jnp.min(jnp.where(masked == m, lane, big), axis=1, keepdims=True)
            oh = lane == ik
            onehots.append(oh)
            e_k.append(ik)
            w_k.append(jnp.sum(jnp.where(oh, scores, 0.0), axis=1, keepdims=True))
            masked = jnp.where(oh, neg, masked)
            sel = sel | oh
        wsum = w_k[0]
        for kk in range(1, TOP_K):
            wsum = wsum + w_k[kk]

        seld = sel.astype(F32)
        rk = jnp.dot(_tri(tm, strict=True).astype(BF16), seld.astype(BF16),
                     preferred_element_type=F32) + carry_ref[...]
        carry_ref[...] = carry_ref[...] + jnp.sum(seld, axis=0, keepdims=True)

        ol = lax.broadcasted_iota(jnp.int32, (tm, LANES), 1)
        packed = jnp.zeros((tm, LANES), F32)
        for kk in range(TOP_K):
            pos = jnp.sum(jnp.where(onehots[kk], rk, 0.0), axis=1, keepdims=True)
            wgt = w_k[kk] / wsum * ROUTED_SCALE
            packed = jnp.where(ol == _ROUTE_E + kk, e_k[kk], packed)
            packed = jnp.where(ol == _ROUTE_POS + kk, pos, packed)
            packed = jnp.where(ol == _ROUTE_W + kk, wgt, packed)
        route_ref[...] = packed


def _router(xpre, g, b, w_router, b_router, t_pad):
    t = xpre.shape[0]
    tm = _RT_TM
    n_tiles = t // tm
    last = n_tiles - 1
    assert t_pad % tm == 0 and t_pad > t
    return pl.pallas_call(
        functools.partial(_router_kernel, n_tiles=n_tiles),
        grid=(t_pad // tm,),
        in_specs=[pl.BlockSpec((tm, D_MODEL), lambda i: (jnp.minimum(i, last), 0)),
                  pl.BlockSpec((1, D_MODEL), lambda i: (0, 0)),
                  pl.BlockSpec((1, D_MODEL), lambda i: (0, 0)),
                  pl.BlockSpec((D_MODEL, N_EXPERTS), lambda i: (0, 0)),
                  pl.BlockSpec((1, N_EXPERTS), lambda i: (0, 0))],
        out_specs=[pl.BlockSpec((tm, D_MODEL), lambda i: (i, 0)),
                   pl.BlockSpec((tm, D_MODEL), lambda i: (i, 0)),
                   pl.BlockSpec((tm, LANES), lambda i: (i, 0)),
                   pl.BlockSpec((1, N_EXPERTS), lambda i: (0, 0))],
        out_shape=[jax.ShapeDtypeStruct((t_pad, D_MODEL), F32),
                   jax.ShapeDtypeStruct((t_pad, D_MODEL), BF16),
                   jax.ShapeDtypeStruct((t_pad, LANES), F32),
                   jax.ShapeDtypeStruct((1, N_EXPERTS), F32)],
        scratch_shapes=[pltpu.VMEM((1, N_EXPERTS), F32)],
        compiler_params=_cparams(("arbitrary",), VMEM_LIMIT),
        name="router",
    )(xpre, g, b, w_router, b_router)


def _row_copy(src_hbm, dst, sem, src_row, dst_row):
    return pltpu.make_async_copy(src_hbm.at[pl.ds(src_row, 1)], dst.at[pl.ds(dst_row, 1)], sem)


def _gather_kernel(nseg_ref, cnt_ref, idx_ref, idxn_ref, x_hbm, out_ref, buf, sem):
    s = pl.program_id(0)
    nseg = nseg_ref[0]

    def fetch(slot, idx, n):
        def zero(g, carry):
            buf[slot, pl.ds(pl.multiple_of(g * 8, 8), 8), :] = jnp.zeros((8, D_MODEL), F32)
            return carry

        lax.fori_loop(n // 8, SEG // 8, zero, 0)

        def issue(r, carry):
            _row_copy(x_hbm, buf.at[slot], sem.at[slot], idx[0, 0, r], r).start()
            return carry

        lax.fori_loop(0, n, issue, 0)

    @pl.when(s == 0)
    def _():
        fetch(0, idx_ref, cnt_ref[0])

    @pl.when(s + 1 < nseg)
    def _():
        fetch((s + 1) % 2, idxn_ref, cnt_ref[s + 1])

    @pl.when(s < nseg)
    def _():
        slot = s % 2

        def drain(r, carry):
            _row_copy(x_hbm, buf.at[slot], sem.at[slot], 0, r).wait()
            return carry

        lax.fori_loop(0, cnt_ref[s], drain, 0)
        out_ref[...] = buf[slot].astype(BF16)

    @pl.when(s >= nseg)
    def _():
        out_ref[...] = jnp.zeros_like(out_ref)


def _gather_rows(nseg, seg_cnt, slot_tok3, x1, nseg_max):
    last = nseg_max - 1
    return pl.pallas_call(
        _gather_kernel,
        grid_spec=pltpu.PrefetchScalarGridSpec(
            num_scalar_prefetch=2,
            grid=(nseg_max,),
            in_specs=[pl.BlockSpec((1, 1, SEG), lambda s, n, c: (s, 0, 0), memory_space=pltpu.SMEM),
                      pl.BlockSpec((1, 1, SEG), lambda s, n, c: (jnp.minimum(s + 1, last), 0, 0),
                                   memory_space=pltpu.SMEM),
                      pl.BlockSpec(memory_space=pl.ANY)],
            out_specs=pl.BlockSpec((SEG, D_MODEL), lambda s, n, c: (s, 0)),
            scratch_shapes=[pltpu.VMEM((2, SEG, D_MODEL), F32), pltpu.SemaphoreType.DMA((2,))],
        ),
        out_shape=jax.ShapeDtypeStruct((nseg_max * SEG, D_MODEL), BF16),
        compiler_params=_cparams(("arbitrary",), VMEM_LIMIT),
        name="gather_rows",
    )(nseg, seg_cnt, slot_tok3, slot_tok3, x1)


def _experts_kernel(nseg_ref, e_ref, y0_ref, nb_ref, m_ref, z0_ref, nz_ref, x_ref, wg_hbm, wu_hbm, wd_hbm, y_hbm,
                    h_ref, wgb_ref, wub_ref, wdb_ref, abuf, bbuf, ybuf, sem_a, sem_b, sem_y, sem_z, *, nseg_max):
    s = pl.program_id(0)
    nseg = nseg_ref[0]
    valid = s < nseg
    e = e_ref[s]

    def a_copy(ex, c):
        w = wg_hbm if c % 2 == 0 else wu_hbm
        slot = c % A_SLOTS
        return pltpu.make_async_copy(w.at[ex, :, pl.ds((c // 2) * F_CHUNK, F_CHUNK)], abuf.at[slot],
                                     sem_a.at[slot])

    def b_copy(ex, n):
        slot = n % B_SLOTS
        return pltpu.make_async_copy(wd_hbm.at[ex, :, pl.ds(n * D_CHUNK, D_CHUNK)], bbuf.at[slot],
                                     sem_b.at[slot])

    def y_copy(ys, half, yblk, n, sem):
        return pltpu.make_async_copy(ybuf.at[ys, pl.ds(half * YBLK, YBLK), :],
                                     y_hbm.at[pl.ds(yblk * YBLK, YBLK), pl.ds(n * D_CHUNK, D_CHUNK)], sem)

    def wait_out(ys, nbv):
        y_copy(ys, 0, 0, 0, sem_y.at[ys]).wait()

        @pl.when(nbv == 2)
        def _():
            y_copy(ys, 1, 0, 0, sem_y.at[ys]).wait()

    @pl.when(s == 0)
    def _():
        for c in range(A_SLOTS):
            a_copy(e, c).start()
        for n in range(B_SLOTS):
            b_copy(e, n).start()

    def segment(m):
        halves = -(-m // YBLK)
        has_next = s + 1 < nseg
        e_next = e_ref[jnp.minimum(s + 1, nseg_max - 1)]
        nbs = nb_ref[s]
        y0 = y0_ref[s]
        x = x_ref[0:m, :]

        for f in range(N_FCH):
            for c, dst in ((2 * f, wgb_ref), (2 * f + 1, wub_ref)):
                a_copy(e, c).wait()
                dst[...] = abuf[c % A_SLOTS].astype(BF16)
                nxt = c + A_SLOTS
                if nxt < N_ACH:
                    a_copy(e, nxt).start()
                else:
                    @pl.when(has_next)
                    def _():
                        a_copy(e_next, nxt - N_ACH).start()
            a = jnp.dot(x, wgb_ref[...], preferred_element_type=F32)
            u = jnp.dot(x, wub_ref[...], preferred_element_type=F32)
            h_ref[0:m, f * F_CHUNK:(f + 1) * F_CHUNK] = (_silu(a) * u).astype(BF16)

        h = h_ref[0:m, :]
        for n in range(N_DCH):
            b_copy(e, n).wait()
            wdb_ref[...] = bbuf[n % B_SLOTS].astype(BF16)
            nxt = n + B_SLOTS
            if nxt < N_DCH:
                b_copy(e, nxt).start()
            else:
                @pl.when(has_next)
                def _():
                    b_copy(e_next, nxt - N_DCH).start()
            ys = n % 2
            if n >= 2:
                for half in range(halves):
                    y_copy(ys, half, 0, 0, sem_y.at[ys]).wait()
            else:
                @pl.when(s > 0)
                def _():
                    wait_out(ys, nb_ref[jnp.maximum(s - 1, 0)])
                if m < halves * YBLK:
                    ybuf[ys, m:halves * YBLK, :] = jnp.zeros((halves * YBLK - m, D_CHUNK), F32)
            ybuf[ys, 0:m, :] = jnp.dot(h, wdb_ref[...], preferred_element_type=F32)
            for half in range(halves):
                y_copy(ys, half, y0 + half, n, sem_y.at[ys]).start()

        @pl.when(s == nseg_max - 1)
        def _():
            wait_out(0, nbs)
            wait_out(1, nbs)

    @pl.when(valid)
    def _():
        m_steps = m_ref[s]
        for mv in range(YBLK // ROW_STEP, SEG // ROW_STEP + 1):
            @pl.when(m_steps == mv)
            def _():
                segment(mv * ROW_STEP)

    @pl.when(s == nseg)
    def _():
        nbp = nb_ref[jnp.maximum(s - 1, 0)]
        wait_out(0, nbp)
        wait_out(1, nbp)

    @pl.when(jnp.logical_not(valid))
    def _():
        nz = nz_ref[s]
        z0 = z0_ref[s]

        @pl.when(nz > 0)
        def _():
            ybuf[0] = jnp.zeros((SEG, D_CHUNK), F32)
            for wait in (False, True):
                for q in range(ZFILL_PER_STEP):
                    @pl.when(q < nz)
                    def _():
                        for n in range(N_DCH):
                            cp = y_copy(0, 0, z0 + q, n, sem_z)
                            cp.wait() if wait else cp.start()


def _experts(sched, x_sorted, w_gate, w_up, w_down, nseg_max, n_yrows):
    def im_x(s, nseg, *_):
        return (jnp.minimum(s, nseg[0] - 1), 0)

    any_spec = pl.BlockSpec(memory_space=pl.ANY)
    return pl.pallas_call(
        functools.partial(_experts_kernel, nseg_max=nseg_max),
        grid_spec=pltpu.PrefetchScalarGridSpec(
            num_scalar_prefetch=7,
            grid=(nseg_max,),
            in_specs=[pl.BlockSpec((SEG, D_MODEL), im_x), any_spec, any_spec, any_spec],
            out_specs=any_spec,
            scratch_shapes=[pltpu.VMEM((SEG, EXPERT_DIM), BF16),
                            pltpu.VMEM((D_MODEL, F_CHUNK), BF16),
                            pltpu.VMEM((D_MODEL, F_CHUNK), BF16),
                            pltpu.VMEM((EXPERT_DIM, D_CHUNK), BF16),
                            pltpu.VMEM((A_SLOTS, D_MODEL, F_CHUNK), F32),
                            pltpu.VMEM((B_SLOTS, EXPERT_DIM, D_CHUNK), F32),
                            pltpu.VMEM((2, SEG, D_CHUNK), F32),
                            pltpu.SemaphoreType.DMA((A_SLOTS,)),
                            pltpu.SemaphoreType.DMA((B_SLOTS,)),
                            pltpu.SemaphoreType.DMA((2,)),
                            pltpu.SemaphoreType.DMA(())],
        ),
        out_shape=jax.ShapeDtypeStruct((n_yrows, D_MODEL), F32),
        compiler_params=_cparams(("arbitrary",), VMEM_LIMIT),
        name="experts",
    )(sched["nseg"], sched["e"], sched["y0"], sched["nb"], sched["m"], sched["z0"], sched["nz"],
      x_sorted, w_gate, w_up, w_down)


def _pick(onehot, arr):
    return jnp.sum(jnp.where(onehot, arr, 0), axis=-1)


def _routed_schedule(cnt, nseg_max, nyb_max):
    nsg = (cnt + SEG - 1) // SEG
    nyb = (cnt + YBLK - 1) // YBLK
    seg_end = jnp.cumsum(nsg)
    seg_start = seg_end - nsg
    yb_start = jnp.cumsum(nyb) - nyb
    nseg = seg_end[-1]
    nyb_tot = jnp.sum(nyb)
    sidx = jnp.arange(nseg_max, dtype=jnp.int32)
    sc = jnp.minimum(sidx, nseg - 1)
    seg_e = jnp.minimum(jnp.sum((seg_end[None, :] <= sc[:, None]).astype(jnp.int32), axis=1), N_EXPERTS - 1)
    oh = seg_e[:, None] == jnp.arange(N_EXPERTS, dtype=jnp.int32)[None, :]
    loc = sc - _pick(oh, seg_start)
    live = sidx < nseg
    spare = sidx - nseg
    z0 = nyb_tot + ZFILL_PER_STEP * spare
    seg_cnt = jnp.where(live, jnp.clip(_pick(oh, cnt) - SEG * loc, 0, SEG), 0).astype(jnp.int32)
    sched = {
        "nseg": nseg.reshape(1).astype(jnp.int32),
        "e": seg_e.astype(jnp.int32),
        "y0": (_pick(oh, yb_start) + (SEG // YBLK) * loc).astype(jnp.int32),
        "nb": jnp.where(live, jnp.clip(_pick(oh, nyb) - (SEG // YBLK) * loc, 0, SEG // YBLK), 0).astype(jnp.int32),
        "m": jnp.clip((seg_cnt + ROW_STEP - 1) // ROW_STEP, YBLK // ROW_STEP, SEG // ROW_STEP).astype(jnp.int32),
        "z0": jnp.clip(z0, 0, nyb_max - 1).astype(jnp.int32),
        "nz": jnp.where(live, 0, jnp.clip(nyb_max - z0, 0, ZFILL_PER_STEP)).astype(jnp.int32),
    }
    return sched, seg_cnt, seg_start, yb_start


def _dense_schedule(nseg):
    sidx = jnp.arange(nseg, dtype=jnp.int32)
    zero = jnp.zeros((nseg,), jnp.int32)
    return {"nseg": jnp.full((1,), nseg, jnp.int32), "e": zero, "y0": (SEG // YBLK) * sidx,
            "nb": jnp.full((nseg,), SEG // YBLK, jnp.int32), "m": jnp.full((nseg,), SEG // ROW_STEP, jnp.int32),
            "z0": zero, "nz": zero}


def _combine_kernel(slot_ref, slotn_ref, w_ref, ysh_ref, x1_ref, g_ref, b_ref, y_hbm, out_ref, buf, sem,
                    *, n_steps):
    i = pl.program_id(0)
    tm = COMB_TM
    n = tm * TOP_K

    def issue(slot, idx):
        def body(j, carry):
            _row_copy(y_hbm, buf.at[slot], sem.at[slot], idx[0, 0, j], j).start()
            return carry

        lax.fori_loop(0, n, body, 0, unroll=8)

    @pl.when(i == 0)
    def _():
        issue(0, slot_ref)

    @pl.when(i + 1 < n_steps)
    def _():
        issue((i + 1) % 2, slotn_ref)

    cur = i % 2

    def drain(j, carry):
        _row_copy(y_hbm, buf.at[cur], sem.at[cur], 0, j).wait()
        return carry

    lax.fori_loop(0, n, drain, 0, unroll=8)
    w = w_ref[...]
    acc = w[:, 0:1] * buf[cur, 0:tm, :]
    for kk in range(1, TOP_K):
        acc = acc + w[:, kk:kk + 1] * buf[cur, kk * tm:(kk + 1) * tm, :]
    f = acc + ysh_ref[...]
    out_ref[...] = _ln_rows(ALPHA * x1_ref[...] + f, g_ref[...], b_ref[...])


def _combine(slots3, w, ysh, x1, g, b, y_slots, t):
    tm = COMB_TM
    n_steps = t // tm
    last = n_steps - 1
    return pl.pallas_call(
        functools.partial(_combine_kernel, n_steps=n_steps),
        grid=(n_steps,),
        in_specs=[pl.BlockSpec((1, 1, tm * TOP_K), lambda i: (i, 0, 0), memory_space=pltpu.SMEM),
                  pl.BlockSpec((1, 1, tm * TOP_K), lambda i: (jnp.minimum(i + 1, last), 0, 0),
                               memory_space=pltpu.SMEM),
                  pl.BlockSpec((tm, TOP_K), lambda i: (i, 0)),
                  pl.BlockSpec((tm, D_MODEL), lambda i: (i, 0)),
                  pl.BlockSpec((tm, D_MODEL), lambda i: (i, 0)),
                  pl.BlockSpec((1, D_MODEL), lambda i: (0, 0)),
                  pl.BlockSpec((1, D_MODEL), lambda i: (0, 0)),
                  pl.BlockSpec(memory_space=pl.ANY)],
        out_specs=pl.BlockSpec((tm, D_MODEL), lambda i: (i, 0)),
        out_shape=jax.ShapeDtypeStruct((t, D_MODEL), F32),
        scratch_shapes=[pltpu.VMEM((2, tm * TOP_K, D_MODEL), F32), pltpu.SemaphoreType.DMA((2,))],
        compiler_params=_cparams(("arbitrary",), VMEM_LIMIT),
        name="combine",
    )(slots3, slots3, w, ysh, x1, g, b, y_slots)


def _moe(x1, x1b, route, counts, t_tok, t_pad, w_gate, w_up, w_down, w_sh_gate, w_sh_up, w_sh_down, g2, b2):
    n_assign = t_tok * TOP_K
    nseg_max = (n_assign + N_EXPERTS * (SEG - 1)) // SEG
    nyb_max = (n_assign + N_EXPERTS * (YBLK - 1)) // YBLK
    eidx = route[:t_tok, _ROUTE_E:_ROUTE_E + TOP_K].T.astype(jnp.int32)
    pos = route[:t_tok, _ROUTE_POS:_ROUTE_POS + TOP_K].T.astype(jnp.int32)
    wts = route[:t_tok, _ROUTE_W:_ROUTE_W + TOP_K]
    sched, seg_cnt, seg_start, yb_start = _routed_schedule(counts[0].astype(jnp.int32), nseg_max, nyb_max)
    oh_e = eidx[None] == jnp.arange(N_EXPERTS, dtype=jnp.int32)[:, None, None]

    def per_assignment(table):
        return jnp.sum(jnp.where(oh_e, table[:, None, None], 0), axis=0)

    xslot = per_assignment(seg_start) * SEG + pos
    yslot = per_assignment(yb_start) * YBLK + pos
    tok = jnp.broadcast_to(jnp.arange(t_tok, dtype=jnp.int32)[None, :], (TOP_K, t_tok))
    slot_tok = jnp.zeros((nseg_max * SEG,), jnp.int32).at[xslot.reshape(-1)].set(tok.reshape(-1))
    x_sorted = _gather_rows(sched["nseg"], seg_cnt, slot_tok.reshape(nseg_max, 1, SEG), x1, nseg_max)

    y_slots = _experts(sched, x_sorted, w_gate, w_up, w_down, nseg_max, nyb_max * YBLK)
    ysh = _experts(_dense_schedule(t_pad // SEG), x1b, w_sh_gate, w_sh_up, w_sh_down, t_pad // SEG, t_pad)

    n_ct = t_tok // COMB_TM
    slots3 = yslot.reshape(TOP_K, n_ct, COMB_TM).transpose(1, 0, 2).reshape(n_ct, 1, COMB_TM * TOP_K)
    return _combine(slots3, wts, ysh, x1, g2, b2, y_slots, t_tok)


def kernel(x_prompt, x_sample, state_gla, state_conv, meta_tokens, ln_emb_g, ln_emb_b, w_in, w_alpha_up, b_alpha, conv_w, gla_norm_g, w_o, ln1_g, ln1_b, w_router, b_router, w_exp_gate, w_exp_up, w_exp_down, w_sh_gate, w_sh_up, w_sh_down, ln2_g, ln2_b):
    bp, seq, d = x_prompt.shape
    bs = x_sample.shape[0]
    assert d == D_MODEL and x_sample.shape[1] == 1 and w_in.shape[0] == 1
    tp = bp * seq
    row_s = tp
    row_m = tp + bs
    t_tok = tp + bs
    t_all = -(-(row_m + N_META) // 256) * 256
    assert seq % 1024 == 0 and bs == 128 and t_tok % 640 == 0 and t_all % 768 == 0

    row = lambda a: a.reshape(1, -1)

    x_all = jnp.concatenate([x_prompt.reshape(tp, d), x_sample.reshape(bs, d), meta_tokens.astype(F32),
                             jnp.zeros((t_all - row_m - N_META, d), F32)], axis=0)
    lr0 = COL_V + GLA_HEADS * GLA_DV
    w_in0 = w_in[0]
    w_main = _repack_w_in(w_in, lr0)
    w_lr = jnp.pad(w_in0[:, lr0:lr0 + GATE_RANK], ((0, 0), (0, LANES - GATE_RANK))).astype(BF16)
    w_au = jnp.pad(w_alpha_up[0], ((0, LANES - GATE_RANK), (0, 0)))
    xn, xnb, la = _ln_in(x_all, row(ln_emb_g), row(ln_emb_b), w_lr, w_au, row(b_alpha[0]))
    proj = _in_proj(xnb, w_main)

    gain = row(gla_norm_g[0])
    o_p, gla_p = _gla_prompt(proj, la, gain, bp, seq, row_m)

    def cols(a):
        return a.reshape(bs // _SAMP_GRP, _SAMP_GRP, GLA_HEADS, GLA_DK).transpose(0, 2, 3, 1)

    ps = proj[row_s:row_s + bs]
    gla_s, o_s = _gla_sample(state_gla[0], cols(ps[:, COL_Q:COL_Q + QK_WIDTH]),
                             cols(ps[:, COL_K:COL_K + QK_WIDTH]), cols(la[row_s:row_s + bs]),
                             proj, gain, bs, row_s)

    cw = conv_w[0]
    m_p, cs_p = _merge_prompt(o_p, proj, cw, bp, seq, row_m)
    m_s, u_s = _merge_sample(o_s, proj, cw, state_conv[0].reshape(bs, (CONV_K - 1) * d), bs, row_s)
    conv_p = cs_p.reshape(bp, 8, d)[:, 8 - (CONV_K - 1):, :]
    conv_s = jnp.stack([state_conv[0][:, 1, :], u_s], axis=1)

    merged = jnp.concatenate([m_p, m_s], axis=0)
    xpre = _out_proj(merged, w_o[0].astype(BF16), xn)
    t_pad = -(-t_tok // SEG) * SEG
    x1, x1b, route, counts = _router(xpre, row(ln1_g[0]), row(ln1_b[0]), w_router[0], row(b_router[0]), t_pad)

    y = _moe(x1, x1b, route, counts, t_tok, t_pad, w_exp_gate[0], w_exp_up[0], w_exp_down[0],
             w_sh_gate, w_sh_up, w_sh_down, row(ln2_g[0]), row(ln2_b[0]))

    y_prompt = y[:tp].reshape(bp, seq, d)
    y_sample = y[tp:].reshape(bs, 1, d)
    return (y_prompt, y_sample, gla_p[None], conv_p[None], gla_s[None], conv_s[None])
```
